```python
import math
import jax, jax.numpy as jnp
from jax import lax
import numpy as np

D_MODEL = 1024
BATCH = 8
SEQ = 4096
DEPTH = 1

HEAD_DIM = 64
MIX_WIDTH = D_MODEL
ATTN_WIDTH = 3 * MIX_WIDTH // 4
N_ATTN_HEADS = ATTN_WIDTH // HEAD_DIM
CONV_WIDTH = MIX_WIDTH - ATTN_WIDTH
CONV_KERNEL = 31
DILATION_PATTERNS = ((128, 1), (512, 4), (2048, 16))
ROPE_DIM = HEAD_DIM // 4
ROPE_THETA = 500000.0
D_FF = 2816
FFN_CONV_KERNEL = 3
IN_PROJ_WIDTH = 3 * ATTN_WIDTH + 2 * CONV_WIDTH
DEEPNORM_ALPHA = (2.0 * DEPTH) ** 0.25
DEEPNORM_BETA = (8.0 * DEPTH) ** -0.25
LN_EPS = 1e-5

kernel_name = "hybrid_dilated_attn_conformer_convffn_deepnorm"


def layer_norm(x, g, b):
    xf = x.astype(jnp.float32)
    mu = jnp.mean(xf, axis=-1, keepdims=True)
    var = jnp.mean(jnp.square(xf - mu), axis=-1, keepdims=True)
    return ((xf - mu) * lax.rsqrt(var + LN_EPS) * g.astype(jnp.float32) + b.astype(jnp.float32)).astype(x.dtype)


def depthwise_conv(x, w, b):
    width = w.shape[0]
    pad = (width - 1) // 2
    y = lax.conv_general_dilated(
        x, w[:, None, :].astype(x.dtype), window_strides=(1,),
        padding=[(pad, width - 1 - pad)],
        dimension_numbers=("NWC", "WIO", "NWC"),
        feature_group_count=x.shape[-1])
    return y + b.astype(x.dtype)


def apply_partial_rope(t, positions):
    half = ROPE_DIM // 2
    inv_freq = ROPE_THETA ** (-jnp.arange(half, dtype=jnp.float32) / half)
    ang = positions.astype(jnp.float32)[:, None, :, None] * inv_freq
    cos, sin = jnp.cos(ang), jnp.sin(ang)
    tf = t.astype(jnp.float32)
    x1, x2 = tf[..., :half], tf[..., half:ROPE_DIM]
    out = jnp.concatenate([x1 * cos - x2 * sin, x2 * cos + x1 * sin, tf[..., ROPE_DIM:]], axis=-1)
    return out.astype(t.dtype)


def dilated_band_attention(q, k, v, dilation, radius):
    B, H, S, E = q.shape
    L = S // dilation
    blk = radius
    nblk = -(-L // blk)
    Lp = nblk * blk

    def to_residue(t):
        return t.reshape(B, H, L, dilation, E).transpose(0, 1, 3, 2, 4)

    qr, kr, vr = to_residue(q), to_residue(k), to_residue(v)
    qb = jnp.pad(qr, ((0, 0),) * 3 + ((0, Lp - L), (0, 0))).reshape(B, H, dilation, nblk, blk, E)

    def key_windows(t):
        tp = jnp.pad(t, ((0, 0),) * 3 + ((blk, Lp - L + blk), (0, 0)))
        tp = tp.reshape(B, H, dilation, nblk + 2, blk, E)
        return jnp.concatenate([tp[:, :, :, :-2], tp[:, :, :, 1:-1], tp[:, :, :, 2:]], axis=4)

    kw, vw = key_windows(kr), key_windows(vr)
    scores = jnp.einsum("bhrnqe,bhrnke->bhrnqk", qb.astype(jnp.float32), kw.astype(jnp.float32)) * (E ** -0.5)

    q_idx = jnp.arange(nblk)[:, None, None] * blk + jnp.arange(blk)[None, :, None]
    k_idx = (jnp.arange(nblk)[:, None, None] - 1) * blk + jnp.arange(3 * blk)[None, None, :]
    rel = k_idx - q_idx
    valid = ((jnp.abs(rel) <= radius) & (k_idx >= 0) & (k_idx < L)) | (rel == 0)
    scores = jnp.where(valid, scores, -jnp.inf)

    lse = jax.nn.logsumexp(scores, axis=-1)
    probs = jnp.exp(scores - lse[..., None])
    o = jnp.einsum("bhrnqk,bhrnke->bhrnqe", probs, vw.astype(jnp.float32))
    o = o.reshape(B, H, dilation, Lp, E)[:, :, :, :L].transpose(0, 1, 3, 2, 4).reshape(B, H, S, E)
    lse = lse.reshape(B, H, dilation, Lp)[..., :L].transpose(0, 1, 3, 2).reshape(B, H, S)
    return o, lse


def hybrid_mixer(x, positions, w_in, b_glu, conv_w, conv_b, conv_ln_g, conv_ln_b, w_out):
    B, S, _ = x.shape
    proj = x @ w_in
    q, k, v, c_val, c_gate = jnp.split(
        proj, [ATTN_WIDTH, 2 * ATTN_WIDTH, 3 * ATTN_WIDTH, 3 * ATTN_WIDTH + CONV_WIDTH], axis=-1)

    def heads(t):
        return t.reshape(B, S, N_ATTN_HEADS, HEAD_DIM).transpose(0, 2, 1, 3)

    q = apply_partial_rope(heads(q), positions)
    k = apply_partial_rope(heads(k), positions)
    v = heads(v)

    outs, lses = [], []
    for window, dilation in DILATION_PATTERNS:
        o, lse = dilated_band_attention(q, k, v, dilation, window // (2 * dilation))
        outs.append(o)
        lses.append(lse)
    weights = jax.nn.softmax(jnp.stack(lses, axis=0), axis=0)
    attn = jnp.einsum("pbhs,pbhse->bshe", weights, jnp.stack(outs, axis=0))
    attn = attn.reshape(B, S, ATTN_WIDTH).astype(x.dtype)

    u = (c_val + b_glu[:CONV_WIDTH]) * jax.nn.sigmoid(c_gate + b_glu[CONV_WIDTH:])
    u = depthwise_conv(u, conv_w, conv_b)
    u = jax.nn.silu(layer_norm(u, conv_ln_g, conv_ln_b))

    mixed = jnp.concatenate([attn, u.astype(x.dtype)], axis=-1)
    return mixed @ w_out


def conv_gated_mlp(x, w_ffn_in, ffn_conv_w, ffn_conv_b, w_ffn_out):
    h = x @ w_ffn_in
    gate, up = jnp.split(h, [D_FF], axis=-1)
    gate = depthwise_conv(gate, ffn_conv_w, ffn_conv_b)
    return (jax.nn.silu(gate) * up) @ w_ffn_out


def setup_inputs(seed: int = 0) -> dict:
    key = jax.random.key(seed)
    ks = jax.random.split(key, 20)
    f32 = jnp.float32

    def nrm(k, shape, scale):
        return jax.random.normal(k, shape, f32) * scale

    x = jax.random.normal(ks[0], (BATCH, SEQ, D_MODEL), f32)
    positions = jnp.broadcast_to(jnp.arange(SEQ, dtype=jnp.int32)[None, :], (BATCH, SEQ))
    return {
        "x": x,
        "positions": positions,
        "w_in": nrm(ks[1], (DEPTH, D_MODEL, IN_PROJ_WIDTH), D_MODEL ** -0.5),
        "b_glu": nrm(ks[2], (DEPTH, 2 * CONV_WIDTH), 0.02),
        "conv_w": nrm(ks[3], (DEPTH, CONV_KERNEL, CONV_WIDTH), CONV_KERNEL ** -0.5),
        "conv_b": nrm(ks[4], (DEPTH, CONV_WIDTH), 0.02),
        "conv_ln_g": 1.0 + nrm(ks[5], (DEPTH, CONV_WIDTH), 0.02),
        "conv_ln_b": nrm(ks[6], (DEPTH, CONV_WIDTH), 0.02),
        "w_out": nrm(ks[7], (DEPTH, MIX_WIDTH, D_MODEL), DEEPNORM_BETA * MIX_WIDTH ** -0.5),
        "ln1_g": 1.0 + nrm(ks[8], (DEPTH, D_MODEL), 0.02),
        "ln1_b": nrm(ks[9], (DEPTH, D_MODEL), 0.02),
        "w_ffn_in": nrm(ks[10], (DEPTH, D_MODEL, 2 * D_FF), D_MODEL ** -0.5),
        "ffn_conv_w": nrm(ks[11], (DEPTH, FFN_CONV_KERNEL, D_FF), FFN_CONV_KERNEL ** -0.5),
        "ffn_conv_b": nrm(ks[12], (DEPTH, D_FF), 0.02),
        "w_ffn_out": nrm(ks[13], (DEPTH, D_FF, D_MODEL), DEEPNORM_BETA * D_FF ** -0.5),
        "ln2_g": 1.0 + nrm(ks[14], (DEPTH, D_MODEL), 0.02),
        "ln2_b": nrm(ks[15], (DEPTH, D_MODEL), 0.02),
    }


def reference(x, positions, w_in, b_glu, conv_w, conv_b, conv_ln_g, conv_ln_b, w_out,
              ln1_g, ln1_b, w_ffn_in, ffn_conv_w, ffn_conv_b, w_ffn_out, ln2_g, ln2_b):
    for l in range(DEPTH):
        m = hybrid_mixer(x, positions, w_in[l], b_glu[l], conv_w[l], conv_b[l],
                         conv_ln_g[l], conv_ln_b[l], w_out[l])
        x = layer_norm(DEEPNORM_ALPHA * x + m, ln1_g[l], ln1_b[l])
        f = conv_gated_mlp(x, w_ffn_in[l], ffn_conv_w[l], ffn_conv_b[l], w_ffn_out[l])
        x = layer_norm(DEEPNORM_ALPHA * x + f, ln2_g[l], ln2_b[l])
    return x
```

```python
import functools

import jax
import jax.numpy as jnp
from jax import lax
from jax.experimental import pallas as pl
from jax.experimental.pallas import tpu as pltpu

F32 = jnp.float32
BF16 = jnp.bfloat16

HEAD_DIM = 64
ROPE_DIM = HEAD_DIM // 4
ROPE_HALF = ROPE_DIM // 2
ROPE_THETA = 500000.0
CONV_KERNEL = 31
CONV_PAD = (CONV_KERNEL - 1) // 2
FFN_CONV_KERNEL = 3
DILATIONS = (1, 4, 16)
RADIUS = 64
LN_EPS = 1e-5

LANES = 128
SUBLANES = 8
GROUPS = DILATIONS[-1]
VMEM_LIMIT_BYTES = 56 * 1024 * 1024

ROW_TILE = 512
FFN_CHUNK = 256


def _layer_norm(y, g, b):
    mu = jnp.mean(y, axis=-1, keepdims=True)
    yc = y - mu
    var = jnp.mean(yc * yc, axis=-1, keepdims=True)
    return yc * lax.rsqrt(var + LN_EPS) * g + b


def _silu(y):
    return y * (1.0 / (1.0 + jnp.exp(-y)))


def _in_proj_kernel(x_ref, pos_ref, freq_ref, w_ref, bglu_ref, q_ref, k_ref, v_ref, u_ref, *, attn_width, conv_width):
    xb = x_ref[...].astype(BF16)
    ang = pos_ref[...] * freq_ref[...]
    cos = jnp.cos(ang)
    sin = jnp.sin(ang)
    d = lax.broadcasted_iota(jnp.int32, (1, LANES), 1) & (HEAD_DIM - 1)
    c_t = jnp.where(d < ROPE_DIM, cos, 1.0)
    a_t = jnp.where(d < ROPE_HALF, -sin, 0.0)
    b_t = jnp.where((d >= ROPE_HALF) & (d < ROPE_DIM), sin, 0.0)

    def rope(t):
        return t * c_t + pltpu.roll(t, LANES - ROPE_HALF, 1) * a_t + pltpu.roll(t, ROPE_HALF, 1) * b_t

    scale = HEAD_DIM ** -0.5
    q = jnp.dot(xb, w_ref[:, 0:attn_width], preferred_element_type=F32)
    for i in range(attn_width // LANES):
        q_ref[:, i * LANES:(i + 1) * LANES] = rope(q[:, i * LANES:(i + 1) * LANES]) * scale
    k = jnp.dot(xb, w_ref[:, attn_width:2 * attn_width], preferred_element_type=F32)
    for i in range(attn_width // LANES):
        k_ref[:, i * LANES:(i + 1) * LANES] = rope(k[:, i * LANES:(i + 1) * LANES])
    v_ref[...] = jnp.dot(xb, w_ref[:, 2 * attn_width:3 * attn_width], preferred_element_type=F32)
    glu = jnp.dot(xb, w_ref[:, 3 * attn_width:3 * attn_width + 2 * conv_width], preferred_element_type=F32)
    glu = glu + bglu_ref[...]
    c_val = glu[:, 0:conv_width]
    c_gate = glu[:, conv_width:2 * conv_width]
    u_ref[...] = c_val * (1.0 / (1.0 + jnp.exp(-c_gate)))


def _in_proj(x, pos_lanes, freq_lanes, w_in, b_glu, attn_width, conv_width):
    B, S, D = x.shape
    n_in = w_in.shape[1]
    tm = ROW_TILE
    row = lambda b, t: (b, t, 0)
    const = lambda b, t: (0, 0)
    return pl.pallas_call(
        functools.partial(_in_proj_kernel, attn_width=attn_width, conv_width=conv_width),
        grid=(B, S // tm),
        in_specs=[
            pl.BlockSpec((None, tm, D), row),
            pl.BlockSpec((None, tm, LANES), row),
            pl.BlockSpec((1, LANES), const),
            pl.BlockSpec((D, n_in), const),
            pl.BlockSpec((1, 2 * conv_width), const),
        ],
        out_specs=[
            pl.BlockSpec((None, tm, attn_width), row),
            pl.BlockSpec((None, tm, attn_width), row),
            pl.BlockSpec((None, tm, attn_width), row),
            pl.BlockSpec((None, tm, conv_width), row),
        ],
        out_shape=[
            jax.ShapeDtypeStruct((B, S, attn_width), F32),
            jax.ShapeDtypeStruct((B, S, attn_width), F32),
            jax.ShapeDtypeStruct((B, S, attn_width), F32),
            jax.ShapeDtypeStruct((B, S, conv_width), F32),
        ],
        compiler_params=pltpu.CompilerParams(
            dimension_semantics=("arbitrary", "arbitrary"), vmem_limit_bytes=VMEM_LIMIT_BYTES),
        name="in_proj",
    )(x, pos_lanes, freq_lanes, w_in, b_glu)


def _band_block(q_blk, k_blk, v_blk, valid, lane_lo):
    stats = []
    for lane_mask in (lane_lo, jnp.logical_not(lane_lo)):
        qh = jnp.where(lane_mask, q_blk, 0.0).astype(BF16)
        s = lax.dot_general(qh, k_blk, (((1,), (1,)), ((), ())), preferred_element_type=F32)
        s = jnp.where(valid, s, -jnp.inf)
        m = jnp.max(s, axis=1, keepdims=True)
        p = jnp.exp(s - m)
        l = jnp.sum(p, axis=1, keepdims=True)
        pv = jnp.dot(p.astype(BF16), v_blk, preferred_element_type=F32)
        stats.append((m, l, pv))
    (m0, l0, pv0), (m1, l1, pv1) = stats
    shape = pv0.shape
    m = jnp.where(lane_lo, jnp.broadcast_to(m0, shape), jnp.broadcast_to(m1, shape))
    l = jnp.where(lane_lo, jnp.broadcast_to(l0, shape), jnp.broadcast_to(l1, shape))
    pv = jnp.where(lane_lo, pv0, pv1)
    return m, l, pv


def _merge(m_a, l_a, acc_a, m_b, l_b, acc_b):
    m = jnp.maximum(m_a, m_b)
    e_a = jnp.exp(m_a - m)
    e_b = jnp.exp(m_b - m)
    return m, l_a * e_a + l_b * e_b, acc_a * e_a + acc_b * e_b


def _attention_kernel(q_ref, k_ref, v_ref, o_ref,
                      q_res, k_res, v_res, k_nat, v_nat,
                      acc_nat, m_nat, l_nat, acc_res, m_res, l_res, *, seq):
    per_group = seq // GROUPS
    lane_lo = lax.broadcasted_iota(jnp.int32, (1, LANES), 1) < HEAD_DIM

    for r in range(GROUPS):
        q_res[r] = q_ref[pl.ds(r, per_group, stride=GROUPS), :]
        k_res[r] = k_ref[pl.ds(r, per_group, stride=GROUPS), :].astype(BF16)
        v_res[r] = v_ref[pl.ds(r, per_group, stride=GROUPS), :].astype(BF16)
    k_nat[...] = k_ref[...].astype(BF16)
    v_nat[...] = v_ref[...].astype(BF16)

    qb1, kb1 = 2 * RADIUS, 4 * RADIUS

    def dil1_body(j, carry):
        i0 = pl.multiple_of(j * qb1, qb1)
        ws = pl.multiple_of(jnp.clip(i0 - RADIUS, 0, seq - kb1), RADIUS)
        rel = (lax.broadcasted_iota(jnp.int32, (qb1, kb1), 1)
               - lax.broadcasted_iota(jnp.int32, (qb1, kb1), 0) + (ws - i0))
        valid = jnp.abs(rel) <= RADIUS
        q_blk = q_ref[pl.ds(i0, qb1), :]
        m, l, pv = _band_block(q_blk, k_nat[pl.ds(ws, kb1), :], v_nat[pl.ds(ws, kb1), :], valid, lane_lo)
        m_nat[pl.ds(i0, qb1), :] = m
        l_nat[pl.ds(i0, qb1), :] = l
        acc_nat[pl.ds(i0, qb1), :] = pv
        return carry

    lax.fori_loop(0, seq // qb1, dil1_body, 0)

    def dil16_body(r, carry):
        rel = (lax.broadcasted_iota(jnp.int32, (per_group, per_group), 1)
               - lax.broadcasted_iota(jnp.int32, (per_group, per_group), 0))
        valid = jnp.abs(rel) <= RADIUS
        m, l, pv = _band_block(q_res[r], k_res[r], v_res[r], valid, lane_lo)
        m_res[r] = m
        l_res[r] = l
        acc_res[r] = pv
        return carry

    lax.fori_loop(0, GROUPS, dil16_body, 0)

    n_a = GROUPS // DILATIONS[1]
    qm, km = qb1 // n_a, kb1 // n_a
    blocks_per_c = per_group // qm

    def dil4_body(t, carry):
        c = t // blocks_per_c
        m0 = pl.multiple_of((t % blocks_per_c) * qm, qm)
        ms = pl.multiple_of(jnp.clip(m0 - RADIUS // n_a, 0, per_group - km), RADIUS // n_a)
        qi = lax.broadcasted_iota(jnp.int32, (qb1, kb1), 0)
        ki = lax.broadcasted_iota(jnp.int32, (qb1, kb1), 1)
        rel = (n_a * ((ki & (km - 1)) - (qi & (qm - 1)) + (ms - m0))
               + (ki >> (km.bit_length() - 1)) - (qi >> (qm.bit_length() - 1)))
        valid = jnp.abs(rel) <= RADIUS
        q_blk = jnp.concatenate([q_res[n_a * a + c, pl.ds(m0, qm), :] for a in range(n_a)], axis=0)
        k_blk = jnp.concatenate([k_res[n_a * a + c, pl.ds(ms, km), :] for a in range(n_a)], axis=0)
        v_blk = jnp.concatenate([v_res[n_a * a + c, pl.ds(ms, km), :] for a in range(n_a)], axis=0)
        m, l, pv = _band_block(q_blk, k_blk, v_blk, valid, lane_lo)
        for a in range(n_a):
            g = n_a * a + c
            rows = slice(a * qm, (a + 1) * qm)
            m_new, l_new, acc_new = _merge(
                m_res[g, pl.ds(m0, qm), :], l_res[g, pl.ds(m0, qm), :], acc_res[g, pl.ds(m0, qm), :],
                m[rows], l[rows], pv[rows])
            m_res[g, pl.ds(m0, qm), :] = m_new
            l_res[g, pl.ds(m0, qm), :] = l_new
            acc_res[g, pl.ds(m0, qm), :] = acc_new
        return carry

    lax.fori_loop(0, DILATIONS[1] * blocks_per_c, dil4_body, 0)

    for r in range(GROUPS):
        sel = pl.ds(r, per_group, stride=GROUPS)
        _, l, acc = _merge(m_nat[sel, :], l_nat[sel, :], acc_nat[sel, :], m_res[r], l_res[r], acc_res[r])
        o_ref[sel, :] = acc / l


def _attention(q, k, v):
    B, S, W = q.shape
    per_group = S // GROUPS
    blk = pl.BlockSpec((None, S, LANES), lambda b, h: (b, 0, h))
    res_bf16 = pltpu.VMEM((GROUPS, per_group, LANES), BF16)
    res_f32 = pltpu.VMEM((GROUPS, per_group, LANES), F32)
    nat_bf16 = pltpu.VMEM((S, LANES), BF16)
    nat_f32 = pltpu.VMEM((S, LANES), F32)
    return pl.pallas_call(
        functools.partial(_attention_kernel, seq=S),
        grid=(B, W // LANES),
        in_specs=[blk, blk, blk],
        out_specs=blk,
        out_shape=jax.ShapeDtypeStruct((B, S, W), F32),
        scratch_shapes=[res_f32, res_bf16, res_bf16, nat_bf16, nat_bf16,
                        nat_f32, nat_f32, nat_f32, res_f32, res_f32, res_f32],
        compiler_params=pltpu.CompilerParams(
            dimension_semantics=("arbitrary", "arbitrary"), vmem_limit_bytes=VMEM_LIMIT_BYTES),
        name="attention",
    )(q, k, v)


def _mix_kernel(attn_ref, up_ref, u_ref, un_ref, x_ref, w_attn_ref, w_conv_ref, cw_ref, cb_ref, cg_ref, cbeta_ref,
                g1_ref, b1_ref, o_ref, u_win, u_act, *, alpha):
    t = pl.program_id(1)
    tm, conv_width = u_act.shape
    halo = up_ref.shape[0]

    u_win[0:halo, :] = jnp.where(t > 0, up_ref[...], 0.0)
    u_win[halo:halo + tm, :] = u_ref[...]
    u_win[halo + tm:2 * halo + tm, :] = jnp.where(t < pl.num_programs(1) - 1, un_ref[...], 0.0)

    chunk = 64
    for i in range(tm // chunk):
        acc = jnp.zeros((chunk, conv_width), F32)
        for j in range(CONV_KERNEL):
            start = i * chunk + (halo - CONV_PAD) + j
            acc = acc + cw_ref[j:j + 1, :] * u_win[start:start + chunk, :]
        y = _layer_norm(acc + cb_ref[...], cg_ref[...], cbeta_ref[...])
        u_act[i * chunk:(i + 1) * chunk, :] = _silu(y).astype(BF16)

    mixed = jnp.dot(attn_ref[...].astype(BF16), w_attn_ref[...], preferred_element_type=F32)
    mixed = mixed + jnp.dot(u_act[...], w_conv_ref[...], preferred_element_type=F32)
    o_ref[...] = _layer_norm(alpha * x_ref[...] + mixed, g1_ref[...], b1_ref[...])


def _mix(attn, u, x, w_attn, w_conv, conv_w, conv_b, conv_ln_g, conv_ln_b, ln_g, ln_b, alpha):
    B, S, D = x.shape
    attn_width = attn.shape[-1]
    conv_width = u.shape[-1]
    tm = ROW_TILE
    halo = 2 * SUBLANES
    assert halo >= CONV_PAD
    per_tile = tm // halo
    n_halo_blocks = S // halo
    row = lambda b, t: (b, t, 0)
    const = lambda b, t: (0, 0)
    return pl.pallas_call(
        functools.partial(_mix_kernel, alpha=alpha),
        grid=(B, S // tm),
        in_specs=[
            pl.BlockSpec((None, tm, attn_width), row),
            pl.BlockSpec((None, halo, conv_width), lambda b, t: (b, jnp.maximum(t * per_tile - 1, 0), 0)),
            pl.BlockSpec((None, tm, conv_width), row),
            pl.BlockSpec((None, halo, conv_width),
                         lambda b, t: (b, jnp.minimum((t + 1) * per_tile, n_halo_blocks - 1), 0)),
            pl.BlockSpec((None, tm, D), row),
            pl.BlockSpec((attn_width, D), const),
            pl.BlockSpec((conv_width, D), const),
            pl.BlockSpec((CONV_KERNEL, conv_width), const),
            pl.BlockSpec((1, conv_width), const),
            pl.BlockSpec((1, conv_width), const),
            pl.BlockSpec((1, conv_width), const),
            pl.BlockSpec((1, D), const),
            pl.BlockSpec((1, D), const),
        ],
        out_specs=pl.BlockSpec((None, tm, D), row),
        out_shape=jax.ShapeDtypeStruct((B, S, D), F32),
        scratch_shapes=[pltpu.VMEM((tm + 2 * halo, conv_width), F32), pltpu.VMEM((tm, conv_width), BF16)],
        compiler_params=pltpu.CompilerParams(
            dimension_semantics=("arbitrary", "arbitrary"), vmem_limit_bytes=VMEM_LIMIT_BYTES),
        name="mix",
    )(attn, u, u, u, x, w_attn, w_conv, conv_w, conv_b, conv_ln_g, conv_ln_b, ln_g, ln_b)


def _ffn_kernel(xp_ref, x_ref, xn_ref, wg_ref, wu_ref, wo_ref, cw_ref, cb_ref, g2_ref, b2_ref, o_ref,
                g_buf, acc_buf, *, alpha):
    t = pl.program_id(1)
    tm = x_ref.shape[0]
    n_chunks = wg_ref.shape[0]
    x_t = x_ref[...]
    prev = jnp.where(t > 0, xp_ref[...], 0.0)
    nxt = jnp.where(t < pl.num_programs(1) - 1, xn_ref[...], 0.0)
    xb = x_t.astype(BF16)
    hb = jnp.concatenate([prev, x_t, nxt], axis=0).astype(BF16)
    acc_buf[...] = jnp.zeros_like(acc_buf)

    def chunk_body(c, carry):
        g_buf[...] = jnp.dot(hb, wg_ref[c], preferred_element_type=F32)
        up = jnp.dot(xb, wu_ref[c], preferred_element_type=F32)
        cw = cw_ref[c]
        gate = (cw[0:1, :] * g_buf[pl.ds(SUBLANES - 1, tm), :]
                + cw[1:2, :] * g_buf[pl.ds(SUBLANES, tm), :]
                + cw[2:3, :] * g_buf[pl.ds(SUBLANES + 1, tm), :]
                + cb_ref[c])
        act = (_silu(gate) * up).astype(BF16)
        acc_buf[...] += jnp.dot(act, wo_ref[c], preferred_element_type=F32)
        return carry

    lax.fori_loop(0, n_chunks, chunk_body, 0)
    o_ref[...] = _layer_norm(alpha * x_t + acc_buf[...], g2_ref[...], b2_ref[...])


def _ffn(x1, w_gate, w_up, w_out, conv_w, conv_b, ln_g, ln_b, alpha):
    B, S, D = x1.shape
    n_chunks, _, chunk = w_gate.shape
    tm = ROW_TILE
    per_tile = tm // SUBLANES
    n_halo_blocks = S // SUBLANES
    row = lambda b, t: (b, t, 0)
    const2 = lambda b, t: (0, 0)
    const3 = lambda b, t: (0, 0, 0)
    return pl.pallas_call(
        functools.partial(_ffn_kernel, alpha=alpha),
        grid=(B, S // tm),
        in_specs=[
            pl.BlockSpec((None, SUBLANES, D), lambda b, t: (b, jnp.maximum(t * per_tile - 1, 0), 0)),
            pl.BlockSpec((None, tm, D), row),
            pl.BlockSpec((None, SUBLANES, D), lambda b, t: (b, jnp.minimum((t + 1) * per_tile, n_halo_blocks - 1), 0)),
            pl.BlockSpec((n_chunks, D, chunk), const3),
            pl.BlockSpec((n_chunks, D, chunk), const3),
            pl.BlockSpec((n_chunks, chunk, D), const3),
            pl.BlockSpec((n_chunks, SUBLANES, chunk), const3),
            pl.BlockSpec((n_chunks, 1, chunk), const3),
            pl.BlockSpec((1, D), const2),
            pl.BlockSpec((1, D), const2),
        ],
        out_specs=pl.BlockSpec((None, tm, D), row),
        out_shape=jax.ShapeDtypeStruct((B, S, D), F32),
        scratch_shapes=[pltpu.VMEM((tm + 2 * SUBLANES, chunk), F32), pltpu.VMEM((tm, D), F32)],
        compiler_params=pltpu.CompilerParams(
            dimension_semantics=("arbitrary", "arbitrary"), vmem_limit_bytes=VMEM_LIMIT_BYTES),
        name="ffn",
    )(x1, x1, x1, w_gate, w_up, w_out, conv_w, conv_b, ln_g, ln_b)


def kernel(x, positions, w_in, b_glu, conv_w, conv_b, conv_ln_g, conv_ln_b, w_out, ln1_g, ln1_b,
           w_ffn_in, ffn_conv_w, ffn_conv_b, w_ffn_out, ln2_g, ln2_b):
    B, S, D = x.shape
    depth = w_in.shape[0]
    conv_width = conv_w.shape[-1]
    attn_width = w_out.shape[1] - conv_width
    d_ff = w_ffn_out.shape[1]
    n_chunks = d_ff // FFN_CHUNK
    alpha = (2.0 * depth) ** 0.25
    assert S % (GROUPS * 4 * RADIUS // DILATIONS[1]) == 0 and S % ROW_TILE == 0
    assert attn_width % LANES == 0 and d_ff % FFN_CHUNK == 0 and w_in.shape[-1] == 3 * attn_width + 2 * conv_width

    inv_freq = ROPE_THETA ** (-jnp.arange(ROPE_HALF, dtype=F32) / ROPE_HALF)
    freq_lanes = inv_freq[jnp.arange(LANES) % ROPE_HALF][None, :]
    pos_lanes = jnp.broadcast_to(positions.astype(F32)[:, :, None], (B, S, LANES))

    for l in range(depth):
        q, k, v, u = _in_proj(x, pos_lanes, freq_lanes, w_in[l].astype(BF16), b_glu[l][None, :],
                              attn_width, conv_width)
        attn = _attention(q, k, v)
        w_o = w_out[l].astype(BF16)
        x = _mix(attn, u, x, w_o[:attn_width], w_o[attn_width:], conv_w[l], conv_b[l][None, :],
                 conv_ln_g[l][None, :], conv_ln_b[l][None, :], ln1_g[l][None, :], ln1_b[l][None, :], alpha)
        w_fi = w_ffn_in[l].astype(BF16)
        w_gate = w_fi[:, :d_ff].reshape(D, n_chunks, FFN_CHUNK).transpose(1, 0, 2)
        w_up = w_fi[:, d_ff:].reshape(D, n_chunks, FFN_CHUNK).transpose(1, 0, 2)
        w_down = w_ffn_out[l].astype(BF16).reshape(n_chunks, FFN_CHUNK, D)
        cw = jnp.pad(ffn_conv_w[l].reshape(FFN_CONV_KERNEL, n_chunks, FFN_CHUNK).transpose(1, 0, 2),
                     ((0, 0), (0, SUBLANES - FFN_CONV_KERNEL), (0, 0)))
        cb = ffn_conv_b[l].reshape(n_chunks, 1, FFN_CHUNK)
        x = _ffn(x, w_gate, w_up, w_down, cw, cb, ln2_g[l][None, :], ln2_b[l][None, :], alpha)
    return x
```

```python
import functools
import math

import jax
import jax.numpy as jnp
import numpy as np
from jax import lax
from jax.experimental import pallas as pl
from jax.experimental.pallas import tpu as pltpu

F32 = jnp.float32
BF16 = jnp.bfloat16

HEAD_DIM = 64
ROPE_DIM = HEAD_DIM // 4
ROPE_HALF = ROPE_DIM // 2
ROPE_THETA = 500000.0
CONV_KERNEL = 31
CONV_PAD = (CONV_KERNEL - 1) // 2
FFN_CONV_KERNEL = 3
DILATIONS = (1, 4, 16)
RADIUS = 64
LN_EPS = 1e-5

LANES = 128
SUBLANES = 8
GROUPS = DILATIONS[-1]
VMEM_LIMIT_BYTES = 56 * 1024 * 1024

ROW_TILE = 512
FFN_CHUNK = 256
ATTN_UNROLL = 8


def _layer_norm(y, g, b):
    mu = jnp.mean(y, axis=-1, keepdims=True)
    yc = y - mu
    var = jnp.mean(yc * yc, axis=-1, keepdims=True)
    return yc * lax.rsqrt(var + LN_EPS) * g + b


def _silu(y):
    return y * (1.0 / (1.0 + jnp.exp(-y)))


def _in_proj_kernel(x_ref, pos_ref, freq_ref, w_ref, bglu_ref, q_ref, k_ref, v_ref, u_ref, *, attn_width, conv_width):
    xb = x_ref[...].astype(BF16)
    ang = pos_ref[...] * freq_ref[...]
    cos = jnp.cos(ang)
    sin = jnp.sin(ang)
    d = lax.broadcasted_iota(jnp.int32, (1, LANES), 1) & (HEAD_DIM - 1)
    c_t = jnp.where(d < ROPE_DIM, cos, 1.0)
    a_t = jnp.where(d < ROPE_HALF, -sin, 0.0)
    b_t = jnp.where((d >= ROPE_HALF) & (d < ROPE_DIM), sin, 0.0)

    def rope(t):
        return t * c_t + pltpu.roll(t, LANES - ROPE_HALF, 1) * a_t + pltpu.roll(t, ROPE_HALF, 1) * b_t

    scale = HEAD_DIM ** -0.5 * math.log2(math.e)
    q = jnp.dot(xb, w_ref[:, 0:attn_width], preferred_element_type=F32)
    for i in range(attn_width // LANES):
        q_ref[:, i * LANES:(i + 1) * LANES] = rope(q[:, i * LANES:(i + 1) * LANES]) * scale
    k = jnp.dot(xb, w_ref[:, attn_width:2 * attn_width], preferred_element_type=F32)
    for i in range(attn_width // LANES):
        k_ref[:, i * LANES:(i + 1) * LANES] = rope(k[:, i * LANES:(i + 1) * LANES])
    v_ref[...] = jnp.dot(xb, w_ref[:, 2 * attn_width:3 * attn_width], preferred_element_type=F32)
    glu = jnp.dot(xb, w_ref[:, 3 * attn_width:3 * attn_width + 2 * conv_width], preferred_element_type=F32)
    glu = glu + bglu_ref[...]
    c_val = glu[:, 0:conv_width]
    c_gate = glu[:, conv_width:2 * conv_width]
    u_ref[...] = c_val * (1.0 / (1.0 + jnp.exp(-c_gate)))


def _in_proj(x, pos_lanes, freq_lanes, w_in, b_glu, attn_width, conv_width):
    B, S, D = x.shape
    n_in = w_in.shape[1]
    tm = ROW_TILE
    row = lambda b, t: (b, t, 0)
    const = lambda b, t: (0, 0)
    return pl.pallas_call(
        functools.partial(_in_proj_kernel, attn_width=attn_width, conv_width=conv_width),
        grid=(B, S // tm),
        in_specs=[
            pl.BlockSpec((None, tm, D), row),
            pl.BlockSpec((None, tm, LANES), row),
            pl.BlockSpec((1, LANES), const),
            pl.BlockSpec((D, n_in), const),
            pl.BlockSpec((1, 2 * conv_width), const),
        ],
        out_specs=[
            pl.BlockSpec((None, tm, attn_width), row),
            pl.BlockSpec((None, tm, attn_width), row),
            pl.BlockSpec((None, tm, attn_width), row),
            pl.BlockSpec((None, tm, conv_width), row),
        ],
        out_shape=[
            jax.ShapeDtypeStruct((B, S, attn_width), F32),
            jax.ShapeDtypeStruct((B, S, attn_width), F32),
            jax.ShapeDtypeStruct((B, S, attn_width), F32),
            jax.ShapeDtypeStruct((B, S, conv_width), F32),
        ],
        compiler_params=pltpu.CompilerParams(
            dimension_semantics=("arbitrary", "arbitrary"), vmem_limit_bytes=VMEM_LIMIT_BYTES),
        name="in_proj",
    )(x, pos_lanes, freq_lanes, w_in, b_glu)


QBLK = 2 * RADIUS
KWIN = 4 * RADIUS


def _band_bias():
    rel = np.arange(KWIN)[None, :] - np.arange(QBLK)[:, None]
    tables = [np.where(np.abs(rel - i * RADIUS) <= RADIUS, 0.0, -np.inf) for i in range(3)]
    return jnp.asarray(np.stack(tables), F32)


def _window(i0, length):
    ws = jnp.clip(i0 - RADIUS, 0, length - KWIN)
    return pl.multiple_of(ws, RADIUS), (i0 - ws) // RADIUS


def _band_block(q_blk, k_win, v_win, bias, lane_lo):
    parts = []
    for lane_mask in (lane_lo, jnp.logical_not(lane_lo)):
        qh = jnp.where(lane_mask, q_blk, 0.0).astype(BF16)
        s = lax.dot_general(qh, k_win, (((1,), (1,)), ((), ())), preferred_element_type=F32) + bias
        m = jnp.max(s, axis=1, keepdims=True)
        p = jnp.exp2(s - m).astype(BF16)
        parts.append((m, jnp.dot(p, v_win, preferred_element_type=F32)))
    (m0, pv0), (m1, pv1) = parts
    m = jnp.where(lane_lo, m0, m1)
    l = jnp.where(lane_lo, pv0[:, LANES:], pv1[:, LANES:])
    acc = jnp.where(lane_lo, pv0[:, :LANES], pv1[:, :LANES])
    return m, l, acc


def _merge(m_a, l_a, acc_a, m_b, l_b, acc_b):
    m = jnp.maximum(m_a, m_b)
    e_a = jnp.exp2(m_a - m)
    e_b = jnp.exp2(m_b - m)
    return m, l_a * e_a + l_b * e_b, acc_a * e_a + acc_b * e_b


def _attention_kernel(q_ref, k_ref, v_ref, bias_ref, o_ref,
                      k_nat, v_nat, m_nat, l_nat, acc_nat,
                      q_cls, k_cls32, v_cls32, k_cls, v_cls, k_grp, v_grp,
                      m_cls, l_cls, acc_cls, *, seq, unroll):
    n_cls = DILATIONS[1]
    cls_len = seq // n_cls
    n_grp = GROUPS // n_cls
    grp_len = cls_len // n_grp
    lane_lo = lax.broadcasted_iota(jnp.int32, (1, LANES), 1) < HEAD_DIM

    def band_pass(length, q_src, k_src, v_src, emit):
        def body(j, carry):
            i0 = pl.multiple_of(j * QBLK, QBLK)
            ws, table = _window(i0, length)
            emit(i0, *_band_block(q_src[pl.ds(i0, QBLK), :], k_src[pl.ds(ws, KWIN), :], v_src[pl.ds(ws, KWIN), :],
                                  bias_ref[table], lane_lo))
            return carry
        lax.fori_loop(0, length // QBLK, body, 0, unroll=unroll)

    k_nat[...] = k_ref[...].astype(BF16)
    v_nat[:, :LANES] = v_ref[...].astype(BF16)
    v_nat[:, LANES:] = jnp.ones((seq, LANES), BF16)

    def emit_nat(i0, m, l, acc):
        m_nat[pl.ds(i0, QBLK), :] = m
        l_nat[pl.ds(i0, QBLK), :] = l
        acc_nat[pl.ds(i0, QBLK), :] = acc

    band_pass(seq, q_ref, k_nat, v_nat, emit_nat)

    for c in range(n_cls):
        cls_rows = pl.ds(c, cls_len, stride=n_cls)
        q_cls[c] = q_ref[cls_rows, :]
        k_c = k_ref[cls_rows, :]
        v_c = v_ref[cls_rows, :]
        k_cls32[c] = k_c
        v_cls32[c] = v_c
        k_cls[c] = k_c.astype(BF16)
        v_cls[c, :, :LANES] = v_c.astype(BF16)
        v_cls[c, :, LANES:] = jnp.ones((cls_len, LANES), BF16)

        for a in range(n_grp):
            g = c * n_grp + a
            grp_rows = pl.ds(a, grp_len, stride=n_grp)
            k_grp[g] = k_cls32[c, grp_rows, :].astype(BF16)
            v_grp[g, :, :LANES] = v_cls32[c, grp_rows, :].astype(BF16)
            v_grp[g, :, LANES:] = jnp.ones((grp_len, LANES), BF16)
        for a in range(n_grp):
            g = c * n_grp + a
            for i0 in range(0, grp_len, QBLK):
                ws = min(max(i0 - RADIUS, 0), grp_len - KWIN)
                rows = pl.ds(a + n_grp * i0, QBLK, stride=n_grp)
                m, l, acc = _band_block(q_cls[c, rows, :], k_grp[g, ws:ws + KWIN, :], v_grp[g, ws:ws + KWIN, :],
                                        bias_ref[(i0 - ws) // RADIUS], lane_lo)
                m_cls[c, rows, :] = m
                l_cls[c, rows, :] = l
                acc_cls[c, rows, :] = acc

        for u0 in range(0, cls_len, QBLK):
            ws = min(max(u0 - RADIUS, 0), cls_len - KWIN)
            m, l, acc = _band_block(q_cls[c, u0:u0 + QBLK, :], k_cls[c, ws:ws + KWIN, :], v_cls[c, ws:ws + KWIN, :],
                                    bias_ref[(u0 - ws) // RADIUS], lane_lo)
            m, l, acc = _merge(m, l, acc, m_cls[c, u0:u0 + QBLK, :], l_cls[c, u0:u0 + QBLK, :],
                               acc_cls[c, u0:u0 + QBLK, :])
            nat_rows = pl.ds(c + n_cls * u0, QBLK, stride=n_cls)
            _, l, acc = _merge(m, l, acc, m_nat[nat_rows, :], l_nat[nat_rows, :], acc_nat[nat_rows, :])
            o_ref[nat_rows, :] = acc / l


def _attention(q, k, v):
    B, S, W = q.shape
    n_cls = DILATIONS[1]
    cls_len = S // n_cls
    n_grp = GROUPS // n_cls
    grp_len = cls_len // n_grp
    blk = pl.BlockSpec((None, S, LANES), lambda b, h: (b, 0, h))
    nat_f32 = pltpu.VMEM((S, LANES), F32)
    cls_f32 = pltpu.VMEM((n_cls, cls_len, LANES), F32)
    return pl.pallas_call(
        functools.partial(_attention_kernel, seq=S, unroll=ATTN_UNROLL),
        grid=(B, W // LANES),
        in_specs=[blk, blk, blk, pl.BlockSpec((3, QBLK, KWIN), lambda b, h: (0, 0, 0))],
        out_specs=blk,
        out_shape=jax.ShapeDtypeStruct((B, S, W), F32),
        scratch_shapes=[
            pltpu.VMEM((S, LANES), BF16), pltpu.VMEM((S, 2 * LANES), BF16),
            nat_f32, nat_f32, nat_f32,
            cls_f32, cls_f32, cls_f32,
            pltpu.VMEM((n_cls, cls_len, LANES), BF16), pltpu.VMEM((n_cls, cls_len, 2 * LANES), BF16),
            pltpu.VMEM((GROUPS, grp_len, LANES), BF16), pltpu.VMEM((GROUPS, grp_len, 2 * LANES), BF16),
            cls_f32, cls_f32, cls_f32,
        ],
        compiler_params=pltpu.CompilerParams(
            dimension_semantics=("arbitrary", "arbitrary"), vmem_limit_bytes=VMEM_LIMIT_BYTES),
        name="attention",
    )(q, k, v, _band_bias())


def _mix_kernel(attn_ref, up_ref, u_ref, un_ref, x_ref, w_attn_ref, w_conv_ref, cw_ref, cb_ref, cg_ref, cbeta_ref,
                g1_ref, b1_ref, o_ref, u_win, u_shift, u_act, *, alpha):
    t = pl.program_id(1)
    tm, conv_width = u_act.shape
    halo = up_ref.shape[0]

    u_win[0:halo, :] = jnp.where(t > 0, up_ref[...], 0.0)
    u_win[halo:halo + tm, :] = u_ref[...]
    u_win[halo + tm:2 * halo + tm, :] = jnp.where(t < pl.num_programs(1) - 1, un_ref[...], 0.0)

    span = u_shift.shape[1]
    for k in range(1, SUBLANES):
        u_shift[k] = u_win[k:k + span, :]

    chunk = 64
    for i in range(tm // chunk):
        acc = jnp.zeros((chunk, conv_width), F32)
        for j in range(CONV_KERNEL):
            k = (halo - CONV_PAD + j) % SUBLANES
            start = i * chunk + (halo - CONV_PAD + j) - k
            tap = u_win[start:start + chunk, :] if k == 0 else u_shift[k, start:start + chunk, :]
            acc = acc + cw_ref[j:j + 1, :] * tap
        y = _layer_norm(acc + cb_ref[...], cg_ref[...], cbeta_ref[...])
        u_act[i * chunk:(i + 1) * chunk, :] = _silu(y).astype(BF16)

    mixed = jnp.dot(attn_ref[...].astype(BF16), w_attn_ref[...], preferred_element_type=F32)
    mixed = mixed + jnp.dot(u_act[...], w_conv_ref[...], preferred_element_type=F32)
    o_ref[...] = _layer_norm(alpha * x_ref[...] + mixed, g1_ref[...], b1_ref[...])


def _mix(attn, u, x, w_attn, w_conv, conv_w, conv_b, conv_ln_g, conv_ln_b, ln_g, ln_b, alpha):
    B, S, D = x.shape
    attn_width = attn.shape[-1]
    conv_width = u.shape[-1]
    tm = ROW_TILE
    halo = 2 * SUBLANES
    assert halo >= CONV_PAD
    per_tile = tm // halo
    n_halo_blocks = S // halo
    row = lambda b, t: (b, t, 0)
    const = lambda b, t: (0, 0)
    return pl.pallas_call(
        functools.partial(_mix_kernel, alpha=alpha),
        grid=(B, S // tm),
        in_specs=[
            pl.BlockSpec((None, tm, attn_width), row),
            pl.BlockSpec((None, halo, conv_width), lambda b, t: (b, jnp.maximum(t * per_tile - 1, 0), 0)),
            pl.BlockSpec((None, tm, conv_width), row),
            pl.BlockSpec((None, halo, conv_width),
                         lambda b, t: (b, jnp.minimum((t + 1) * per_tile, n_halo_blocks - 1), 0)),
            pl.BlockSpec((None, tm, D), row),
            pl.BlockSpec((attn_width, D), const),
            pl.BlockSpec((conv_width, D), const),
            pl.BlockSpec((CONV_KERNEL, conv_width), const),
            pl.BlockSpec((1, conv_width), const),
            pl.BlockSpec((1, conv_width), const),
            pl.BlockSpec((1, conv_width), const),
            pl.BlockSpec((1, D), const),
            pl.BlockSpec((1, D), const),
        ],
        out_specs=pl.BlockSpec((None, tm, D), row),
        out_shape=jax.ShapeDtypeStruct((B, S, D), F32),
        scratch_shapes=[pltpu.VMEM((tm + 2 * halo, conv_width), F32),
                        pltpu.VMEM((SUBLANES, tm + 2 * halo - SUBLANES, conv_width), F32),
                        pltpu.VMEM((tm, conv_width), BF16)],
        compiler_params=pltpu.CompilerParams(
            dimension_semantics=("arbitrary", "arbitrary"), vmem_limit_bytes=VMEM_LIMIT_BYTES),
        name="mix",
    )(attn, u, u, u, x, w_attn, w_conv, conv_w, conv_b, conv_ln_g, conv_ln_b, ln_g, ln_b)


def _ffn_kernel(xp_ref, x_ref, xn_ref, wg_ref, wu_ref, wo_ref, cw_ref, cb_ref, g2_ref, b2_ref, o_ref,
                g_buf, acc_buf, *, alpha):
    t = pl.program_id(1)
    tm = x_ref.shape[0]
    n_chunks = wg_ref.shape[0]
    x_t = x_ref[...]
    prev = jnp.where(t > 0, xp_ref[...], 0.0)
    nxt = jnp.where(t < pl.num_programs(1) - 1, xn_ref[...], 0.0)
    xb = x_t.astype(BF16)
    hb = jnp.concatenate([prev, x_t, nxt], axis=0).astype(BF16)
    acc_buf[...] = jnp.zeros_like(acc_buf)

    def chunk_body(c, carry):
        g_buf[...] = jnp.dot(hb, wg_ref[c], preferred_element_type=F32)
        up = jnp.dot(xb, wu_ref[c], preferred_element_type=F32)
        cw = cw_ref[c]
        gate = (cw[0:1, :] * g_buf[pl.ds(SUBLANES - 1, tm), :]
                + cw[1:2, :] * g_buf[pl.ds(SUBLANES, tm), :]
                + cw[2:3, :] * g_buf[pl.ds(SUBLANES + 1, tm), :]
                + cb_ref[c])
        act = (_silu(gate) * up).astype(BF16)
        acc_buf[...] += jnp.dot(act, wo_ref[c], preferred_element_type=F32)
        return carry

    lax.fori_loop(0, n_chunks, chunk_body, 0, unroll=True)
    o_ref[...] = _layer_norm(alpha * x_t + acc_buf[...], g2_ref[...], b2_ref[...])


def _ffn(x1, w_gate, w_up, w_out, conv_w, conv_b, ln_g, ln_b, alpha):
    B, S, D = x1.shape
    n_chunks, _, chunk = w_gate.shape
    tm = ROW_TILE
    per_tile = tm // SUBLANES
    n_halo_blocks = S // SUBLANES
    row = lambda b, t: (b, t, 0)
    const2 = lambda b, t: (0, 0)
    const3 = lambda b, t: (0, 0, 0)
    return pl.pallas_call(
        functools.partial(_ffn_kernel, alpha=alpha),
        grid=(B, S // tm),
        in_specs=[
            pl.BlockSpec((None, SUBLANES, D), lambda b, t: (b, jnp.maximum(t * per_tile - 1, 0), 0)),
            pl.BlockSpec((None, tm, D), row),
            pl.BlockSpec((None, SUBLANES, D), lambda b, t: (b, jnp.minimum((t + 1) * per_tile, n_halo_blocks - 1), 0)),
            pl.BlockSpec((n_chunks, D, chunk), const3),
            pl.BlockSpec((n_chunks, D, chunk), const3),
            pl.BlockSpec((n_chunks, chunk, D), const3),
            pl.BlockSpec((n_chunks, SUBLANES, chunk), const3),
            pl.BlockSpec((n_chunks, 1, chunk), const3),
            pl.BlockSpec((1, D), const2),
            pl.BlockSpec((1, D), const2),
        ],
        out_specs=pl.BlockSpec((None, tm, D), row),
        out_shape=jax.ShapeDtypeStruct((B, S, D), F32),
        scratch_shapes=[pltpu.VMEM((tm + 2 * SUBLANES, chunk), F32), pltpu.VMEM((tm, D), F32)],
        compiler_params=pltpu.CompilerParams(
            dimension_semantics=("arbitrary", "arbitrary"), vmem_limit_bytes=VMEM_LIMIT_BYTES),
        name="ffn",
    )(x1, x1, x1, w_gate, w_up, w_out, conv_w, conv_b, ln_g, ln_b)


def kernel(x, positions, w_in, b_glu, conv_w, conv_b, conv_ln_g, conv_ln_b, w_out, ln1_g, ln1_b,
           w_ffn_in, ffn_conv_w, ffn_conv_b, w_ffn_out, ln2_g, ln2_b):
    B, S, D = x.shape
    depth = w_in.shape[0]
    conv_width = conv_w.shape[-1]
    attn_width = w_out.shape[1] - conv_width
    d_ff = w_ffn_out.shape[1]
    n_chunks = d_ff // FFN_CHUNK
    alpha = (2.0 * depth) ** 0.25
    assert S % (GROUPS * QBLK) == 0 and S // GROUPS >= KWIN and S % ROW_TILE == 0
    assert attn_width % LANES == 0 and d_ff % FFN_CHUNK == 0 and w_in.shape[-1] == 3 * attn_width + 2 * conv_width

    inv_freq = ROPE_THETA ** (-jnp.arange(ROPE_HALF, dtype=F32) / ROPE_HALF)
    freq_lanes = inv_freq[jnp.arange(LANES) % ROPE_HALF][None, :]
    pos_lanes = jnp.broadcast_to(positions.astype(F32)[:, :, None], (B, S, LANES))

    for l in range(depth):
        q, k, v, u = _in_proj(x, pos_lanes, freq_lanes, w_in[l].astype(BF16), b_glu[l][None, :],
                              attn_width, conv_width)
        attn = _attention(q, k, v)
        w_o = w_out[l].astype(BF16)
        x = _mix(attn, u, x, w_o[:attn_width], w_o[attn_width:], conv_w[l], conv_b[l][None, :],
                 conv_ln_g[l][None, :], conv_ln_b[l][None, :], ln1_g[l][None, :], ln1_b[l][None, :], alpha)
        w_fi = w_ffn_in[l].astype(BF16)
        w_gate = w_fi[:, :d_ff].reshape(D, n_chunks, FFN_CHUNK).transpose(1, 0, 2)
        w_up = w_fi[:, d_ff:].reshape(D, n_chunks, FFN_CHUNK).transpose(1, 0, 2)
        w_down = w_ffn_out[l].astype(BF16).reshape(n_chunks, FFN_CHUNK, D)
        cw = jnp.pad(ffn_conv_w[l].reshape(FFN_CONV_KERNEL, n_chunks, FFN_CHUNK).transpose(1, 0, 2),
                     ((0, 0), (0, SUBLANES - FFN_CONV_KERNEL), (0, 0)))
        cb = ffn_conv_b[l].reshape(n_chunks, 1, FFN_CHUNK)
        x = _ffn(x, w_gate, w_up, w_down, cw, cb, ln2_g[l][None, :], ln2_b[l][None, :], alpha)
    return x
```

```python
import functools
import math

import jax
import jax.numpy as jnp
import numpy as np
from jax import lax
from jax.experimental import pallas as pl
from jax.experimental.pallas import tpu as pltpu

F32 = jnp.float32
BF16 = jnp.bfloat16

HEAD_DIM = 64
ROPE_DIM = HEAD_DIM // 4
ROPE_HALF = ROPE_DIM // 2
ROPE_THETA = 500000.0
CONV_KERNEL = 31
CONV_PAD = (CONV_KERNEL - 1) // 2
FFN_CONV_KERNEL = 3
DILATIONS = (1, 4, 16)
RADIUS = 64
LN_EPS = 1e-5

LANES = 128
SUBLANES = 8
GROUPS = DILATIONS[-1]
VMEM_LIMIT_BYTES = 56 * 1024 * 1024

ROW_TILE = 512
FFN_CHUNK = 256
ATTN_UNROLL = 8


def _layer_norm(y, g, b):
    mu = jnp.mean(y, axis=-1, keepdims=True)
    yc = y - mu
    var = jnp.mean(yc * yc, axis=-1, keepdims=True)
    return yc * lax.rsqrt(var + LN_EPS) * g + b


def _silu(y):
    return y * (1.0 / (1.0 + jnp.exp(-y)))


def _in_proj_kernel(x_ref, pos_ref, freq_ref, w_ref, bglu_ref, q_ref, k_ref, v_ref, u_ref, *, attn_width, conv_width):
    xb = x_ref[...].astype(BF16)
    ang = pos_ref[...] * freq_ref[...]
    cos = jnp.cos(ang)
    sin = jnp.sin(ang)
    d = lax.broadcasted_iota(jnp.int32, (1, LANES), 1) & (HEAD_DIM - 1)
    c_t = jnp.where(d < ROPE_DIM, cos, 1.0)
    a_t = jnp.where(d < ROPE_HALF, -sin, 0.0)
    b_t = jnp.where((d >= ROPE_HALF) & (d < ROPE_DIM), sin, 0.0)

    def rope(t):
        return t * c_t + pltpu.roll(t, LANES - ROPE_HALF, 1) * a_t + pltpu.roll(t, ROPE_HALF, 1) * b_t

    scale = HEAD_DIM ** -0.5 * math.log2(math.e)
    q = jnp.dot(xb, w_ref[:, 0:attn_width], preferred_element_type=F32)
    for i in range(attn_width // LANES):
        q_ref[:, i * LANES:(i + 1) * LANES] = rope(q[:, i * LANES:(i + 1) * LANES]) * scale
    k = jnp.dot(xb, w_ref[:, attn_width:2 * attn_width], preferred_element_type=F32)
    for i in range(attn_width // LANES):
        k_ref[:, i * LANES:(i + 1) * LANES] = rope(k[:, i * LANES:(i + 1) * LANES])
    v_ref[...] = jnp.dot(xb, w_ref[:, 2 * attn_width:3 * attn_width], preferred_element_type=F32)
    glu = jnp.dot(xb, w_ref[:, 3 * attn_width:3 * attn_width + 2 * conv_width], preferred_element_type=F32)
    glu = glu + bglu_ref[...]
    c_val = glu[:, 0:conv_width]
    c_gate = glu[:, conv_width:2 * conv_width]
    u_ref[...] = c_val * (1.0 / (1.0 + jnp.exp(-c_gate)))


def _in_proj(x, pos_lanes, freq_lanes, w_in, b_glu, attn_width, conv_width):
    B, S, D = x.shape
    n_in = w_in.shape[1]
    tm = ROW_TILE
    row = lambda b, t: (b, t, 0)
    const = lambda b, t: (0, 0)
    return pl.pallas_call(
        functools.partial(_in_proj_kernel, attn_width=attn_width, conv_width=conv_width),
        grid=(B, S // tm),
        in_specs=[
            pl.BlockSpec((None, tm, D), row),
            pl.BlockSpec((None, tm, LANES), row),
            pl.BlockSpec((1, LANES), const),
            pl.BlockSpec((D, n_in), const),
            pl.BlockSpec((1, 2 * conv_width), const),
        ],
        out_specs=[
            pl.BlockSpec((None, tm, attn_width), row),
            pl.BlockSpec((None, tm, attn_width), row),
            pl.BlockSpec((None, tm, attn_width), row),
            pl.BlockSpec((None, tm, conv_width), row),
        ],
        out_shape=[
            jax.ShapeDtypeStruct((B, S, attn_width), F32),
            jax.ShapeDtypeStruct((B, S, attn_width), F32),
            jax.ShapeDtypeStruct((B, S, attn_width), F32),
            jax.ShapeDtypeStruct((B, S, conv_width), F32),
        ],
        compiler_params=pltpu.CompilerParams(
            dimension_semantics=("arbitrary", "arbitrary"), vmem_limit_bytes=VMEM_LIMIT_BYTES),
        name="in_proj",
    )(x, pos_lanes, freq_lanes, w_in, b_glu)


QBLK = 2 * RADIUS
KWIN = 4 * RADIUS


def _band_bias():
    rel = np.arange(KWIN)[None, :] - np.arange(QBLK)[:, None]
    tables = [np.where(np.abs(rel - i * RADIUS) <= RADIUS, 0.0, -np.inf) for i in range(3)]
    return jnp.asarray(np.stack(tables), F32)


def _window(i0, length):
    ws = jnp.clip(i0 - RADIUS, 0, length - KWIN)
    return pl.multiple_of(ws, RADIUS), (i0 - ws) // RADIUS


def _band_block(q_blk, k_win, v_win, bias, lane_lo):
    parts = []
    for lane_mask in (lane_lo, jnp.logical_not(lane_lo)):
        qh = jnp.where(lane_mask, q_blk, 0.0).astype(BF16)
        s = lax.dot_general(qh, k_win, (((1,), (1,)), ((), ())), preferred_element_type=F32) + bias
        m = jnp.max(s, axis=1, keepdims=True)
        p = jnp.exp2(s - m).astype(BF16)
        parts.append((m, jnp.dot(p, v_win, preferred_element_type=F32)))
    (m0, pv0), (m1, pv1) = parts
    m = jnp.where(lane_lo, m0, m1)
    l = jnp.where(lane_lo, pv0[:, LANES:], pv1[:, LANES:])
    acc = jnp.where(lane_lo, pv0[:, :LANES], pv1[:, :LANES])
    return m, l, acc


def _merge(m_a, l_a, acc_a, m_b, l_b, acc_b):
    m = jnp.maximum(m_a, m_b)
    e_a = jnp.exp2(m_a - m)
    e_b = jnp.exp2(m_b - m)
    return m, l_a * e_a + l_b * e_b, acc_a * e_a + acc_b * e_b


def _attention_kernel(q_ref, k_ref, v_ref, bias_ref, o_ref,
                      k_nat, v_nat, m_nat, l_nat, acc_nat,
                      q_cls, k_cls32, v_cls32, k_cls, v_cls, k_grp, v_grp,
                      m_cls, l_cls, acc_cls, *, seq, unroll):
    n_cls = DILATIONS[1]
    cls_len = seq // n_cls
    n_grp = GROUPS // n_cls
    grp_len = cls_len // n_grp
    lane_lo = lax.broadcasted_iota(jnp.int32, (1, LANES), 1) < HEAD_DIM

    def band_pass(length, q_src, k_src, v_src, emit):
        def body(j, carry):
            i0 = pl.multiple_of(j * QBLK, QBLK)
            ws, table = _window(i0, length)
            emit(i0, *_band_block(q_src[pl.ds(i0, QBLK), :], k_src[pl.ds(ws, KWIN), :], v_src[pl.ds(ws, KWIN), :],
                                  bias_ref[table], lane_lo))
            return carry
        lax.fori_loop(0, length // QBLK, body, 0, unroll=unroll)

    k_nat[...] = k_ref[...].astype(BF16)
    v_nat[:, :LANES] = v_ref[...].astype(BF16)
    v_nat[:, LANES:] = jnp.ones((seq, LANES), BF16)

    def emit_nat(i0, m, l, acc):
        m_nat[pl.ds(i0, QBLK), :] = m
        l_nat[pl.ds(i0, QBLK), :] = l
        acc_nat[pl.ds(i0, QBLK), :] = acc

    band_pass(seq, q_ref, k_nat, v_nat, emit_nat)

    for c in range(n_cls):
        cls_rows = pl.ds(c, cls_len, stride=n_cls)
        q_cls[c] = q_ref[cls_rows, :]
        k_c = k_ref[cls_rows, :]
        v_c = v_ref[cls_rows, :]
        k_cls32[c] = k_c
        v_cls32[c] = v_c
        k_cls[c] = k_c.astype(BF16)
        v_cls[c, :, :LANES] = v_c.astype(BF16)
        v_cls[c, :, LANES:] = jnp.ones((cls_len, LANES), BF16)

        for a in range(n_grp):
            g = c * n_grp + a
            grp_rows = pl.ds(a, grp_len, stride=n_grp)
            k_grp[g] = k_cls32[c, grp_rows, :].astype(BF16)
            v_grp[g, :, :LANES] = v_cls32[c, grp_rows, :].astype(BF16)
            v_grp[g, :, LANES:] = jnp.ones((grp_len, LANES), BF16)
        for a in range(n_grp):
            g = c * n_grp + a
            for i0 in range(0, grp_len, QBLK):
                ws = min(max(i0 - RADIUS, 0), grp_len - KWIN)
                rows = pl.ds(a + n_grp * i0, QBLK, stride=n_grp)
                m, l, acc = _band_block(q_cls[c, rows, :], k_grp[g, ws:ws + KWIN, :], v_grp[g, ws:ws + KWIN, :],
                                        bias_ref[(i0 - ws) // RADIUS], lane_lo)
                m_cls[c, rows, :] = m
                l_cls[c, rows, :] = l
                acc_cls[c, rows, :] = acc

        for u0 in range(0, cls_len, QBLK):
            ws = min(max(u0 - RADIUS, 0), cls_len - KWIN)
            m, l, acc = _band_block(q_cls[c, u0:u0 + QBLK, :], k_cls[c, ws:ws + KWIN, :], v_cls[c, ws:ws + KWIN, :],
                                    bias_ref[(u0 - ws) // RADIUS], lane_lo)
            m, l, acc = _merge(m, l, acc, m_cls[c, u0:u0 + QBLK, :], l_cls[c, u0:u0 + QBLK, :],
                               acc_cls[c, u0:u0 + QBLK, :])
            nat_rows = pl.ds(c + n_cls * u0, QBLK, stride=n_cls)
            _, l, acc = _merge(m, l, acc, m_nat[nat_rows, :], l_nat[nat_rows, :], acc_nat[nat_rows, :])
            o_ref[nat_rows, :] = acc / l


def _attention(q, k, v):
    B, S, W = q.shape
    n_cls = DILATIONS[1]
    cls_len = S // n_cls
    n_grp = GROUPS // n_cls
    grp_len = cls_len // n_grp
    blk = pl.BlockSpec((None, S, LANES), lambda b, h: (b, 0, h))
    nat_f32 = pltpu.VMEM((S, LANES), F32)
    cls_f32 = pltpu.VMEM((n_cls, cls_len, LANES), F32)
    return pl.pallas_call(
        functools.partial(_attention_kernel, seq=S, unroll=ATTN_UNROLL),
        grid=(B, W // LANES),
        in_specs=[blk, blk, blk, pl.BlockSpec((3, QBLK, KWIN), lambda b, h: (0, 0, 0))],
        out_specs=blk,
        out_shape=jax.ShapeDtypeStruct((B, S, W), F32),
        scratch_shapes=[
            pltpu.VMEM((S, LANES), BF16), pltpu.VMEM((S, 2 * LANES), BF16),
            nat_f32, nat_f32, nat_f32,
            cls_f32, cls_f32, cls_f32,
            pltpu.VMEM((n_cls, cls_len, LANES), BF16), pltpu.VMEM((n_cls, cls_len, 2 * LANES), BF16),
            pltpu.VMEM((GROUPS, grp_len, LANES), BF16), pltpu.VMEM((GROUPS, grp_len, 2 * LANES), BF16),
            cls_f32, cls_f32, cls_f32,
        ],
        compiler_params=pltpu.CompilerParams(
            dimension_semantics=("arbitrary", "arbitrary"), vmem_limit_bytes=VMEM_LIMIT_BYTES),
        name="attention",
    )(q, k, v, _band_bias())


def _mix_kernel(attn_ref, up_ref, u_ref, un_ref, x_ref, w_attn_ref, w_conv_ref, cw_ref, cb_ref, cg_ref, cbeta_ref,
                g1_ref, b1_ref, o_ref, u_win, u_shift, u_act, *, alpha):
    t = pl.program_id(1)
    tm, conv_width = u_act.shape
    halo = up_ref.shape[0]

    u_win[0:halo, :] = jnp.where(t > 0, up_ref[...], 0.0)
    u_win[halo:halo + tm, :] = u_ref[...]
    u_win[halo + tm:2 * halo + tm, :] = jnp.where(t < pl.num_programs(1) - 1, un_ref[...], 0.0)

    span = u_shift.shape[1]
    for k in range(1, SUBLANES):
        u_shift[k] = u_win[k:k + span, :]

    chunk = 64
    for i in range(tm // chunk):
        acc = jnp.zeros((chunk, conv_width), F32)
        for j in range(CONV_KERNEL):
            k = (halo - CONV_PAD + j) % SUBLANES
            start = i * chunk + (halo - CONV_PAD + j) - k
            tap = u_win[start:start + chunk, :] if k == 0 else u_shift[k, start:start + chunk, :]
            acc = acc + cw_ref[j:j + 1, :] * tap
        y = _layer_norm(acc + cb_ref[...], cg_ref[...], cbeta_ref[...])
        u_act[i * chunk:(i + 1) * chunk, :] = _silu(y).astype(BF16)

    mixed = jnp.dot(attn_ref[...].astype(BF16), w_attn_ref[...], preferred_element_type=F32)
    mixed = mixed + jnp.dot(u_act[...], w_conv_ref[...], preferred_element_type=F32)
    o_ref[...] = _layer_norm(alpha * x_ref[...] + mixed, g1_ref[...], b1_ref[...])


def _mix(attn, u, x, w_attn, w_conv, conv_w, conv_b, conv_ln_g, conv_ln_b, ln_g, ln_b, alpha):
    B, S, D = x.shape
    attn_width = attn.shape[-1]
    conv_width = u.shape[-1]
    tm = ROW_TILE
    halo = 2 * SUBLANES
    assert halo >= CONV_PAD
    per_tile = tm // halo
    n_halo_blocks = S // halo
    row = lambda b, t: (b, t, 0)
    const = lambda b, t: (0, 0)
    return pl.pallas_call(
        functools.partial(_mix_kernel, alpha=alpha),
        grid=(B, S // tm),
        in_specs=[
            pl.BlockSpec((None, tm, attn_width), row),
            pl.BlockSpec((None, halo, conv_width), lambda b, t: (b, jnp.maximum(t * per_tile - 1, 0), 0)),
            pl.BlockSpec((None, tm, conv_width), row),
            pl.BlockSpec((None, halo, conv_width),
                         lambda b, t: (b, jnp.minimum((t + 1) * per_tile, n_halo_blocks - 1), 0)),
            pl.BlockSpec((None, tm, D), row),
            pl.BlockSpec((attn_width, D), const),
            pl.BlockSpec((conv_width, D), const),
            pl.BlockSpec((CONV_KERNEL, conv_width), const),
            pl.BlockSpec((1, conv_width), const),
            pl.BlockSpec((1, conv_width), const),
            pl.BlockSpec((1, conv_width), const),
            pl.BlockSpec((1, D), const),
            pl.BlockSpec((1, D), const),
        ],
        out_specs=pl.BlockSpec((None, tm, D), row),
        out_shape=jax.ShapeDtypeStruct((B, S, D), F32),
        scratch_shapes=[pltpu.VMEM((tm + 2 * halo, conv_width), F32),
                        pltpu.VMEM((SUBLANES, tm + 2 * halo - SUBLANES, conv_width), F32),
                        pltpu.VMEM((tm, conv_width), BF16)],
        compiler_params=pltpu.CompilerParams(
            dimension_semantics=("arbitrary", "arbitrary"), vmem_limit_bytes=VMEM_LIMIT_BYTES),
        name="mix",
    )(attn, u, u, u, x, w_attn, w_conv, conv_w, conv_b, conv_ln_g, conv_ln_b, ln_g, ln_b)


def _ffn_kernel(xp_ref, x_ref, xn_ref, wi_ref, wo_ref, cw_ref, cb_ref, g2_ref, b2_ref, o_ref,
                act_buf, *, alpha, chunk):
    t = pl.program_id(1)
    tm, d_ff = act_buf.shape
    x_t = x_ref[...]
    prev = jnp.where(t > 0, xp_ref[...], 0.0)
    nxt = jnp.where(t < pl.num_programs(1) - 1, xn_ref[...], 0.0)
    xb = x_t.astype(BF16)
    hb = jnp.concatenate([prev, x_t, nxt], axis=0).astype(BF16)
    rows = tm + 2 * SUBLANES
    mid = slice(SUBLANES, SUBLANES + tm)

    for c0 in range(0, d_ff, chunk):
        cols = slice(c0, c0 + chunk)
        g = jnp.dot(hb, wi_ref[:, cols], preferred_element_type=F32)
        up = jnp.dot(xb, wi_ref[:, d_ff + c0:d_ff + c0 + chunk], preferred_element_type=F32)
        gate = (cw_ref[0:1, cols] * pltpu.roll(g, 1, 0)[mid]
                + cw_ref[1:2, cols] * g[mid]
                + cw_ref[2:3, cols] * pltpu.roll(g, rows - 1, 0)[mid]
                + cb_ref[:, cols])
        act_buf[:, cols] = (_silu(gate) * up).astype(BF16)

    ffn = jnp.dot(act_buf[...], wo_ref[...], preferred_element_type=F32)
    o_ref[...] = _layer_norm(alpha * x_t + ffn, g2_ref[...], b2_ref[...])


def _ffn(x1, w_in, w_out, conv_w, conv_b, ln_g, ln_b, alpha):
    B, S, D = x1.shape
    d_ff = w_out.shape[0]
    tm = ROW_TILE
    per_tile = tm // SUBLANES
    n_halo_blocks = S // SUBLANES
    row = lambda b, t: (b, t, 0)
    const2 = lambda b, t: (0, 0)
    return pl.pallas_call(
        functools.partial(_ffn_kernel, alpha=alpha, chunk=FFN_CHUNK),
        grid=(B, S // tm),
        in_specs=[
            pl.BlockSpec((None, SUBLANES, D), lambda b, t: (b, jnp.maximum(t * per_tile - 1, 0), 0)),
            pl.BlockSpec((None, tm, D), row),
            pl.BlockSpec((None, SUBLANES, D), lambda b, t: (b, jnp.minimum((t + 1) * per_tile, n_halo_blocks - 1), 0)),
            pl.BlockSpec((D, 2 * d_ff), const2),
            pl.BlockSpec((d_ff, D), const2),
            pl.BlockSpec((FFN_CONV_KERNEL, d_ff), const2),
            pl.BlockSpec((1, d_ff), const2),
            pl.BlockSpec((1, D), const2),
            pl.BlockSpec((1, D), const2),
        ],
        out_specs=pl.BlockSpec((None, tm, D), row),
        out_shape=jax.ShapeDtypeStruct((B, S, D), F32),
        scratch_shapes=[pltpu.VMEM((tm, d_ff), BF16)],
        compiler_params=pltpu.CompilerParams(
            dimension_semantics=("arbitrary", "arbitrary"), vmem_limit_bytes=VMEM_LIMIT_BYTES),
        name="ffn",
    )(x1, x1, x1, w_in, w_out, conv_w, conv_b, ln_g, ln_b)


def kernel(x, positions, w_in, b_glu, conv_w, conv_b, conv_ln_g, conv_ln_b, w_out, ln1_g, ln1_b,
           w_ffn_in, ffn_conv_w, ffn_conv_b, w_ffn_out, ln2_g, ln2_b):
    B, S, D = x.shape
    depth = w_in.shape[0]
    conv_width = conv_w.shape[-1]
    attn_width = w_out.shape[1] - conv_width
    d_ff = w_ffn_out.shape[1]
    alpha = (2.0 * depth) ** 0.25
    assert S % (GROUPS * QBLK) == 0 and S // GROUPS >= KWIN and S % ROW_TILE == 0
    assert attn_width % LANES == 0 and d_ff % FFN_CHUNK == 0 and w_in.shape[-1] == 3 * attn_width + 2 * conv_width

    inv_freq = ROPE_THETA ** (-jnp.arange(ROPE_HALF, dtype=F32) / ROPE_HALF)
    freq_lanes = inv_freq[jnp.arange(LANES) % ROPE_HALF][None, :]
    pos_lanes = jnp.broadcast_to(positions.astype(F32)[:, :, None], (B, S, LANES))

    for l in range(depth):
        q, k, v, u = _in_proj(x, pos_lanes, freq_lanes, w_in[l].astype(BF16), b_glu[l][None, :],
                              attn_width, conv_width)
        attn = _attention(q, k, v)
        w_o = w_out[l].astype(BF16)
        x = _mix(attn, u, x, w_o[:attn_width], w_o[attn_width:], conv_w[l], conv_b[l][None, :],
                 conv_ln_g[l][None, :], conv_ln_b[l][None, :], ln1_g[l][None, :], ln1_b[l][None, :], alpha)
        x = _ffn(x, w_ffn_in[l].astype(BF16), w_ffn_out[l].astype(BF16), ffn_conv_w[l], ffn_conv_b[l][None, :],
                 ln2_g[l][None, :], ln2_b[l][None, :], alpha)
    return x
```

```python
import functools
import math

import jax
import jax.numpy as jnp
import numpy as np
from jax import lax
from jax.experimental import pallas as pl
from jax.experimental.pallas import tpu as pltpu

F32 = jnp.float32
BF16 = jnp.bfloat16

HEAD_DIM = 64
ROPE_DIM = HEAD_DIM // 4
ROPE_HALF = ROPE_DIM // 2
ROPE_THETA = 500000.0
CONV_KERNEL = 31
CONV_PAD = (CONV_KERNEL - 1) // 2
FFN_CONV_KERNEL = 3
DILATIONS = (1, 4, 16)
RADIUS = 64
LN_EPS = 1e-5

LANES = 128
SUBLANES = 8
GROUPS = DILATIONS[-1]
VMEM_LIMIT_BYTES = 56 * 1024 * 1024

ROW_TILE = 512
FFN_CHUNK = 256
ATTN_UNROLL = 8


def _layer_norm(y, g, b):
    mu = jnp.mean(y, axis=-1, keepdims=True)
    yc = y - mu
    var = jnp.mean(yc * yc, axis=-1, keepdims=True)
    return yc * lax.rsqrt(var + LN_EPS) * g + b


def _silu(y):
    return y * (1.0 / (1.0 + jnp.exp(-y)))


def _rope_table_kernel(pos_ref, freq_ref, cos_ref, sin_ref):
    ang = pos_ref[...] * freq_ref[...]
    cos_ref[...] = jnp.cos(ang)
    sin_ref[...] = jnp.sin(ang)


def _rope_tables(positions, inv_freq):
    B, S = positions.shape
    per_row = LANES // ROPE_HALF
    pos = jnp.repeat(positions.astype(F32).reshape(B, S // per_row, per_row), ROPE_HALF, axis=-1)
    freq = jnp.tile(inv_freq, per_row)[None, :]
    blk = pl.BlockSpec((None, S // per_row, LANES), lambda b: (b, 0, 0))
    cos, sin = pl.pallas_call(
        _rope_table_kernel,
        grid=(B,),
        in_specs=[blk, pl.BlockSpec((1, LANES), lambda b: (0, 0))],
        out_specs=[blk, blk],
        out_shape=[jax.ShapeDtypeStruct(pos.shape, F32)] * 2,
        name="rope_table",
    )(pos, freq)
    widen = lambda t: jnp.tile(t.reshape(B, S, ROPE_HALF), (1, 1, per_row))
    return widen(cos), widen(sin)


def _in_proj_kernel(xp_ref, x_ref, xn_ref, cos_ref, sin_ref, w_ref, bglu_ref, cw_ref, cb_ref, cg_ref, cbeta_ref,
                    q_ref, k_ref, v_ref, u_ref, u_win, u_shift, *, attn_width, conv_width):
    t = pl.program_id(1)
    tm = x_ref.shape[0]
    halo = xp_ref.shape[0]
    xh = jnp.concatenate([xp_ref[...], x_ref[...], xn_ref[...]], axis=0).astype(BF16)
    xb = xh[halo:halo + tm]
    cos = cos_ref[...]
    sin = sin_ref[...]
    d = lax.broadcasted_iota(jnp.int32, (1, LANES), 1) & (HEAD_DIM - 1)
    c_t = jnp.where(d < ROPE_DIM, cos, 1.0)
    a_t = jnp.where(d < ROPE_HALF, -sin, 0.0)
    b_t = jnp.where((d >= ROPE_HALF) & (d < ROPE_DIM), sin, 0.0)

    def rope(t):
        return t * c_t + pltpu.roll(t, LANES - ROPE_HALF, 1) * a_t + pltpu.roll(t, ROPE_HALF, 1) * b_t

    glu = jnp.dot(xh, w_ref[:, 3 * attn_width:3 * attn_width + 2 * conv_width], preferred_element_type=F32)
    glu = glu + bglu_ref[...]
    u = glu[:, 0:conv_width] * (1.0 / (1.0 + jnp.exp(-glu[:, conv_width:2 * conv_width])))
    u_win[0:halo, :] = jnp.where(t > 0, u[0:halo], 0.0)
    u_win[halo:halo + tm, :] = u[halo:halo + tm]
    u_win[halo + tm:2 * halo + tm, :] = jnp.where(t < pl.num_programs(1) - 1, u[halo + tm:2 * halo + tm], 0.0)

    span = u_shift.shape[1]
    for s in range(1, SUBLANES):
        u_shift[s] = u_win[s:s + span, :]

    scale = HEAD_DIM ** -0.5 * math.log2(math.e)
    width = 2 * LANES

    def project(sec, out_ref, c0):
        col = sec * attn_width + c0
        y = jnp.dot(xb, w_ref[:, col:col + width], preferred_element_type=F32)
        for i in range(0, width, LANES):
            piece = y[:, i:i + LANES]
            if sec == 0:
                piece = rope(piece) * scale
            elif sec == 1:
                piece = rope(piece)
            out_ref[:, c0 + i:c0 + i + LANES] = piece

    todo = [(sec, ref, c0) for sec, ref in enumerate((q_ref, k_ref, v_ref)) for c0 in range(0, attn_width, width)]
    chunk = 64
    n_conv = tm // chunk
    for i in range(n_conv):
        acc = jnp.zeros((chunk, conv_width), F32)
        for j in range(CONV_KERNEL):
            s = (halo - CONV_PAD + j) % SUBLANES
            start = i * chunk + (halo - CONV_PAD + j) - s
            tap = u_win[start:start + chunk, :] if s == 0 else u_shift[s, start:start + chunk, :]
            acc = acc + cw_ref[j:j + 1, :] * tap
        y = _layer_norm(acc + cb_ref[...], cg_ref[...], cbeta_ref[...])
        u_ref[i * chunk:(i + 1) * chunk, :] = _silu(y).astype(BF16)
        for job in todo[i * len(todo) // n_conv:(i + 1) * len(todo) // n_conv]:
            project(*job)


def _in_proj(x, cos, sin, w_in, b_glu, conv_w, conv_b, conv_ln_g, conv_ln_b, attn_width, conv_width):
    B, S, D = x.shape
    n_in = w_in.shape[1]
    tm = ROW_TILE
    halo = 2 * SUBLANES
    assert halo >= CONV_PAD
    per_tile = tm // halo
    n_halo_blocks = S // halo
    row = lambda b, t: (b, t, 0)
    const = lambda b, t: (0, 0)
    return pl.pallas_call(
        functools.partial(_in_proj_kernel, attn_width=attn_width, conv_width=conv_width),
        grid=(B, S // tm),
        in_specs=[
            pl.BlockSpec((None, halo, D), lambda b, t: (b, jnp.maximum(t * per_tile - 1, 0), 0)),
            pl.BlockSpec((None, tm, D), row),
            pl.BlockSpec((None, halo, D), lambda b, t: (b, jnp.minimum((t + 1) * per_tile, n_halo_blocks - 1), 0)),
            pl.BlockSpec((None, tm, LANES), row),
            pl.BlockSpec((None, tm, LANES), row),
            pl.BlockSpec((D, n_in), const),
            pl.BlockSpec((1, 2 * conv_width), const),
            pl.BlockSpec((CONV_KERNEL, conv_width), const),
            pl.BlockSpec((1, conv_width), const),
            pl.BlockSpec((1, conv_width), const),
            pl.BlockSpec((1, conv_width), const),
        ],
        out_specs=[
            pl.BlockSpec((None, tm, attn_width), row),
            pl.BlockSpec((None, tm, attn_width), row),
            pl.BlockSpec((None, tm, attn_width), row),
            pl.BlockSpec((None, tm, conv_width), row),
        ],
        out_shape=[
            jax.ShapeDtypeStruct((B, S, attn_width), F32),
            jax.ShapeDtypeStruct((B, S, attn_width), F32),
            jax.ShapeDtypeStruct((B, S, attn_width), F32),
            jax.ShapeDtypeStruct((B, S, conv_width), BF16),
        ],
        scratch_shapes=[pltpu.VMEM((tm + 2 * halo, conv_width), F32),
                        pltpu.VMEM((SUBLANES, tm + 2 * halo - SUBLANES, conv_width), F32)],
        compiler_params=pltpu.CompilerParams(
            dimension_semantics=("arbitrary", "arbitrary"), vmem_limit_bytes=VMEM_LIMIT_BYTES),
        name="in_proj",
    )(x, x, x, cos, sin, w_in, b_glu, conv_w, conv_b, conv_ln_g, conv_ln_b)


QBLK = 2 * RADIUS
KWIN = 4 * RADIUS


def _band_bias():
    rel = np.arange(KWIN)[None, :] - np.arange(QBLK)[:, None]
    tables = [np.where(np.abs(rel - i * RADIUS) <= RADIUS, 0.0, -np.inf) for i in range(3)]
    return jnp.asarray(np.stack(tables), F32)


def _window(i0, length):
    ws = jnp.clip(i0 - RADIUS, 0, length - KWIN)
    return pl.multiple_of(ws, RADIUS), (i0 - ws) // RADIUS


def _band_block(q_blk, k_win, v_win, bias, lane_lo):
    parts = []
    for lane_mask in (lane_lo, jnp.logical_not(lane_lo)):
        qh = jnp.where(lane_mask, q_blk, 0.0).astype(BF16)
        s = lax.dot_general(qh, k_win, (((1,), (1,)), ((), ())), preferred_element_type=F32) + bias
        m = jnp.max(s, axis=1, keepdims=True)
        p = jnp.exp2(s - m).astype(BF16)
        parts.append((m, jnp.dot(p, v_win, preferred_element_type=F32)))
    (m0, pv0), (m1, pv1) = parts
    m = jnp.where(lane_lo, m0, m1)
    l = jnp.where(lane_lo, pv0[:, LANES:], pv1[:, LANES:])
    acc = jnp.where(lane_lo, pv0[:, :LANES], pv1[:, :LANES])
    return m, l, acc


def _merge(*parts):
    m = functools.reduce(jnp.maximum, [p[0] for p in parts])
    scale = [jnp.exp2(p[0] - m) for p in parts]
    l = functools.reduce(jnp.add, [p[1] * e for p, e in zip(parts, scale)])
    acc = functools.reduce(jnp.add, [p[2] * e for p, e in zip(parts, scale)])
    return m, l, acc


def _attention_kernel(q_ref, k_ref, v_ref, bias_ref, o_ref,
                      k_nat, v_nat, m_nat, l_nat, acc_nat,
                      q_cls, k_cls32, v_cls32, k_cls, v_cls, k_grp, v_grp,
                      m_cls, l_cls, acc_cls, *, seq, unroll):
    n_cls = DILATIONS[1]
    cls_len = seq // n_cls
    n_grp = GROUPS // n_cls
    grp_len = cls_len // n_grp
    lane_lo = lax.broadcasted_iota(jnp.int32, (1, LANES), 1) < HEAD_DIM

    def band_pass(length, q_src, k_src, v_src, emit):
        def body(j, carry):
            i0 = pl.multiple_of(j * QBLK, QBLK)
            ws, table = _window(i0, length)
            emit(i0, *_band_block(q_src[pl.ds(i0, QBLK), :], k_src[pl.ds(ws, KWIN), :], v_src[pl.ds(ws, KWIN), :],
                                  bias_ref[table], lane_lo))
            return carry
        lax.fori_loop(0, length // QBLK, body, 0, unroll=unroll)

    k_nat[...] = k_ref[...].astype(BF16)
    v_nat[:, :LANES] = v_ref[...].astype(BF16)
    v_nat[:, LANES:] = jnp.ones((seq, LANES), BF16)

    def emit_nat(i0, m, l, acc):
        m_nat[pl.ds(i0, QBLK), :] = m
        l_nat[pl.ds(i0, QBLK), :] = l
        acc_nat[pl.ds(i0, QBLK), :] = acc

    band_pass(seq, q_ref, k_nat, v_nat, emit_nat)

    for c in range(n_cls):
        cls_rows = pl.ds(c, cls_len, stride=n_cls)
        q_cls[c] = q_ref[cls_rows, :]
        k_c = k_ref[cls_rows, :]
        v_c = v_ref[cls_rows, :]
        k_cls32[c] = k_c
        v_cls32[c] = v_c
        k_cls[c] = k_c.astype(BF16)
        v_cls[c, :, :LANES] = v_c.astype(BF16)
        v_cls[c, :, LANES:] = jnp.ones((cls_len, LANES), BF16)

        for a in range(n_grp):
            g = c * n_grp + a
            grp_rows = pl.ds(a, grp_len, stride=n_grp)
            k_grp[g] = k_cls32[c, grp_rows, :].astype(BF16)
            v_grp[g, :, :LANES] = v_cls32[c, grp_rows, :].astype(BF16)
            v_grp[g, :, LANES:] = jnp.ones((grp_len, LANES), BF16)
        for a in range(n_grp):
            g = c * n_grp + a
            for i0 in range(0, grp_len, QBLK):
                ws = min(max(i0 - RADIUS, 0), grp_len - KWIN)
                rows = pl.ds(a + n_grp * i0, QBLK, stride=n_grp)
                m, l, acc = _band_block(q_cls[c, rows, :], k_grp[g, ws:ws + KWIN, :], v_grp[g, ws:ws + KWIN, :],
                                        bias_ref[(i0 - ws) // RADIUS], lane_lo)
                m_cls[c, rows, :] = m
                l_cls[c, rows, :] = l
                acc_cls[c, rows, :] = acc

        for u0 in range(0, cls_len, QBLK):
            ws = min(max(u0 - RADIUS, 0), cls_len - KWIN)
            m, l, acc = _band_block(q_cls[c, u0:u0 + QBLK, :], k_cls[c, ws:ws + KWIN, :], v_cls[c, ws:ws + KWIN, :],
                                    bias_ref[(u0 - ws) // RADIUS], lane_lo)
            cls_blk = slice(u0, u0 + QBLK)
            nat_rows = pl.ds(c + n_cls * u0, QBLK, stride=n_cls)
            _, l, acc = _merge((m, l, acc),
                               (m_cls[c, cls_blk, :], l_cls[c, cls_blk, :], acc_cls[c, cls_blk, :]),
                               (m_nat[nat_rows, :], l_nat[nat_rows, :], acc_nat[nat_rows, :]))
            o_ref[nat_rows, :] = acc / l


def _attention(q, k, v):
    B, S, W = q.shape
    n_cls = DILATIONS[1]
    cls_len = S // n_cls
    n_grp = GROUPS // n_cls
    grp_len = cls_len // n_grp
    blk = pl.BlockSpec((None, S, LANES), lambda b, h: (b, 0, h))
    nat_f32 = pltpu.VMEM((S, LANES), F32)
    cls_f32 = pltpu.VMEM((n_cls, cls_len, LANES), F32)
    return pl.pallas_call(
        functools.partial(_attention_kernel, seq=S, unroll=ATTN_UNROLL),
        grid=(B, W // LANES),
        in_specs=[blk, blk, blk, pl.BlockSpec((3, QBLK, KWIN), lambda b, h: (0, 0, 0))],
        out_specs=blk,
        out_shape=jax.ShapeDtypeStruct((B, S, W), F32),
        scratch_shapes=[
            pltpu.VMEM((S, LANES), BF16), pltpu.VMEM((S, 2 * LANES), BF16),
            nat_f32, nat_f32, nat_f32,
            cls_f32, cls_f32, cls_f32,
            pltpu.VMEM((n_cls, cls_len, LANES), BF16), pltpu.VMEM((n_cls, cls_len, 2 * LANES), BF16),
            pltpu.VMEM((GROUPS, grp_len, LANES), BF16), pltpu.VMEM((GROUPS, grp_len, 2 * LANES), BF16),
            cls_f32, cls_f32, cls_f32,
        ],
        compiler_params=pltpu.CompilerParams(
            dimension_semantics=("arbitrary", "arbitrary"), vmem_limit_bytes=VMEM_LIMIT_BYTES),
        name="attention",
    )(q, k, v, _band_bias())


def _mix_kernel(attn_ref, u_ref, x_ref, w_ref, g1_ref, b1_ref, o_ref, *, alpha):
    attn_width = attn_ref.shape[1]
    mixed = jnp.dot(attn_ref[...].astype(BF16), w_ref[0:attn_width, :], preferred_element_type=F32)
    mixed = mixed + jnp.dot(u_ref[...], w_ref[attn_width:, :], preferred_element_type=F32)
    o_ref[...] = _layer_norm(alpha * x_ref[...] + mixed, g1_ref[...], b1_ref[...])


def _mix(attn, u, x, w_out, ln_g, ln_b, alpha):
    B, S, D = x.shape
    attn_width = attn.shape[-1]
    conv_width = u.shape[-1]
    tm = ROW_TILE
    row = lambda b, t: (b, t, 0)
    const = lambda b, t: (0, 0)
    return pl.pallas_call(
        functools.partial(_mix_kernel, alpha=alpha),
        grid=(B, S // tm),
        in_specs=[
            pl.BlockSpec((None, tm, attn_width), row),
            pl.BlockSpec((None, tm, conv_width), row),
            pl.BlockSpec((None, tm, D), row),
            pl.BlockSpec((attn_width + conv_width, D), const),
            pl.BlockSpec((1, D), const),
            pl.BlockSpec((1, D), const),
        ],
        out_specs=pl.BlockSpec((None, tm, D), row),
        out_shape=jax.ShapeDtypeStruct((B, S, D), F32),
        compiler_params=pltpu.CompilerParams(
            dimension_semantics=("arbitrary", "arbitrary"), vmem_limit_bytes=VMEM_LIMIT_BYTES),
        name="mix",
    )(attn, u, x, w_out, ln_g, ln_b)


def _ffn_kernel(xp_ref, x_ref, xn_ref, wi_ref, wo_ref, cw_ref, cb_ref, g2_ref, b2_ref, o_ref,
                act_buf, *, alpha, chunk):
    t = pl.program_id(1)
    tm, d_ff = act_buf.shape
    x_t = x_ref[...]
    prev = jnp.where(t > 0, xp_ref[...], 0.0)
    nxt = jnp.where(t < pl.num_programs(1) - 1, xn_ref[...], 0.0)
    xb = x_t.astype(BF16)
    hb = jnp.concatenate([prev, x_t, nxt], axis=0).astype(BF16)
    rows = tm + 2 * SUBLANES
    mid = slice(SUBLANES, SUBLANES + tm)

    for c0 in range(0, d_ff, chunk):
        cols = slice(c0, c0 + chunk)
        g = jnp.dot(hb, wi_ref[:, cols], preferred_element_type=F32)
        up = jnp.dot(xb, wi_ref[:, d_ff + c0:d_ff + c0 + chunk], preferred_element_type=F32)
        gate = (cw_ref[0:1, cols] * pltpu.roll(g, 1, 0)[mid]
                + cw_ref[1:2, cols] * g[mid]
                + cw_ref[2:3, cols] * pltpu.roll(g, rows - 1, 0)[mid]
                + cb_ref[:, cols])
        act_buf[:, cols] = (_silu(gate) * up).astype(BF16)

    ffn = jnp.dot(act_buf[...], wo_ref[...], preferred_element_type=F32)
    o_ref[...] = _layer_norm(alpha * x_t + ffn, g2_ref[...], b2_ref[...])


def _ffn(x1, w_in, w_out, conv_w, conv_b, ln_g, ln_b, alpha):
    B, S, D = x1.shape
    d_ff = w_out.shape[0]
    tm = ROW_TILE
    per_tile = tm // SUBLANES
    n_halo_blocks = S // SUBLANES
    row = lambda b, t: (b, t, 0)
    const2 = lambda b, t: (0, 0)
    return pl.pallas_call(
        functools.partial(_ffn_kernel, alpha=alpha, chunk=FFN_CHUNK),
        grid=(B, S // tm),
        in_specs=[
            pl.BlockSpec((None, SUBLANES, D), lambda b, t: (b, jnp.maximum(t * per_tile - 1, 0), 0)),
            pl.BlockSpec((None, tm, D), row),
            pl.BlockSpec((None, SUBLANES, D), lambda b, t: (b, jnp.minimum((t + 1) * per_tile, n_halo_blocks - 1), 0)),
            pl.BlockSpec((D, 2 * d_ff), const2),
            pl.BlockSpec((d_ff, D), const2),
            pl.BlockSpec((FFN_CONV_KERNEL, d_ff), const2),
            pl.BlockSpec((1, d_ff), const2),
            pl.BlockSpec((1, D), const2),
            pl.BlockSpec((1, D), const2),
        ],
        out_specs=pl.BlockSpec((None, tm, D), row),
        out_shape=jax.ShapeDtypeStruct((B, S, D), F32),
        scratch_shapes=[pltpu.VMEM((tm, d_ff), BF16)],
        compiler_params=pltpu.CompilerParams(
            dimension_semantics=("arbitrary", "arbitrary"), vmem_limit_bytes=VMEM_LIMIT_BYTES),
        name="ffn",
    )(x1, x1, x1, w_in, w_out, conv_w, conv_b, ln_g, ln_b)


def kernel(x, positions, w_in, b_glu, conv_w, conv_b, conv_ln_g, conv_ln_b, w_out, ln1_g, ln1_b,
           w_ffn_in, ffn_conv_w, ffn_conv_b, w_ffn_out, ln2_g, ln2_b):
    B, S, D = x.shape
    depth = w_in.shape[0]
    conv_width = conv_w.shape[-1]
    attn_width = w_out.shape[1] - conv_width
    d_ff = w_ffn_out.shape[1]
    alpha = (2.0 * depth) ** 0.25
    assert S % (GROUPS * QBLK) == 0 and S // GROUPS >= KWIN and S % ROW_TILE == 0
    assert attn_width % LANES == 0 and d_ff % FFN_CHUNK == 0 and w_in.shape[-1] == 3 * attn_width + 2 * conv_width

    inv_freq = ROPE_THETA ** (-jnp.arange(ROPE_HALF, dtype=F32) / ROPE_HALF)
    cos, sin = _rope_tables(positions, inv_freq)

    for l in range(depth):
        q, k, v, u = _in_proj(x, cos, sin, w_in[l].astype(BF16), b_glu[l][None, :], conv_w[l], conv_b[l][None, :],
                              conv_ln_g[l][None, :], conv_ln_b[l][None, :], attn_width, conv_width)
        attn = _attention(q, k, v)
        x = _mix(attn, u, x, w_out[l].astype(BF16), ln1_g[l][None, :], ln1_b[l][None, :], alpha)
        x = _ffn(x, w_ffn_in[l].astype(BF16), w_ffn_out[l].astype(BF16), ffn_conv_w[l], ffn_conv_b[l][None, :],
                 ln2_g[l][None, :], ln2_b[l][None, :], alpha)
    return x
```

```python
import functools
import math

import jax
import jax.numpy as jnp
import numpy as np
from jax import lax
from jax.experimental import pallas as pl
from jax.experimental.pallas import tpu as pltpu

F32 = jnp.float32
BF16 = jnp.bfloat16

HEAD_DIM = 64
ROPE_DIM = HEAD_DIM // 4
ROPE_HALF = ROPE_DIM // 2
ROPE_THETA = 500000.0
CONV_KERNEL = 31
CONV_PAD = (CONV_KERNEL - 1) // 2
FFN_CONV_KERNEL = 3
DILATIONS = (1, 4, 16)
RADIUS = 64
LN_EPS = 1e-5

LANES = 128
SUBLANES = 8
GROUPS = DILATIONS[-1]
VMEM_LIMIT_BYTES = 56 * 1024 * 1024

ROW_TILE = 512
FFN_CHUNK = 256
ATTN_UNROLL = 16


def _layer_norm(y, g, b):
    mu = jnp.mean(y, axis=-1, keepdims=True)
    yc = y - mu
    var = jnp.mean(yc * yc, axis=-1, keepdims=True)
    return yc * lax.rsqrt(var + LN_EPS) * g + b


def _silu(y):
    return y * (1.0 / (1.0 + jnp.exp(-y)))


def _in_proj_kernel(x_ref, pos_ref, freq_ref, w_ref, bglu_ref, q_ref, k_ref, v_ref, u_ref, *, attn_width, conv_width):
    xb = x_ref[...].astype(BF16)
    ang = pos_ref[...] * freq_ref[...]
    cos = jnp.cos(ang)
    sin = jnp.sin(ang)
    d = lax.broadcasted_iota(jnp.int32, (1, LANES), 1) & (HEAD_DIM - 1)
    c_t = jnp.where(d < ROPE_DIM, cos, 1.0)
    a_t = jnp.where(d < ROPE_HALF, -sin, 0.0)
    b_t = jnp.where((d >= ROPE_HALF) & (d < ROPE_DIM), sin, 0.0)

    def rope(t):
        return t * c_t + pltpu.roll(t, LANES - ROPE_HALF, 1) * a_t + pltpu.roll(t, ROPE_HALF, 1) * b_t

    scale = HEAD_DIM ** -0.5 * math.log2(math.e)
    q = jnp.dot(xb, w_ref[:, 0:attn_width], preferred_element_type=F32)
    for i in range(attn_width // LANES):
        q_ref[:, i * LANES:(i + 1) * LANES] = rope(q[:, i * LANES:(i + 1) * LANES]) * scale
    k = jnp.dot(xb, w_ref[:, attn_width:2 * attn_width], preferred_element_type=F32)
    for i in range(attn_width // LANES):
        k_ref[:, i * LANES:(i + 1) * LANES] = rope(k[:, i * LANES:(i + 1) * LANES])
    v_ref[...] = jnp.dot(xb, w_ref[:, 2 * attn_width:3 * attn_width], preferred_element_type=F32)
    glu = jnp.dot(xb, w_ref[:, 3 * attn_width:3 * attn_width + 2 * conv_width], preferred_element_type=F32)
    glu = glu + bglu_ref[...]
    c_val = glu[:, 0:conv_width]
    c_gate = glu[:, conv_width:2 * conv_width]
    u_ref[...] = c_val * (1.0 / (1.0 + jnp.exp(-c_gate)))


def _in_proj(x, pos_lanes, freq_lanes, w_in, b_glu, attn_width, conv_width):
    B, S, D = x.shape
    n_in = w_in.shape[1]
    tm = ROW_TILE
    row = lambda b, t: (b, t, 0)
    const = lambda b, t: (0, 0)
    return pl.pallas_call(
        functools.partial(_in_proj_kernel, attn_width=attn_width, conv_width=conv_width),
        grid=(B, S // tm),
        in_specs=[
            pl.BlockSpec((None, tm, D), row),
            pl.BlockSpec((None, tm, LANES), row),
            pl.BlockSpec((1, LANES), const),
            pl.BlockSpec((D, n_in), const),
            pl.BlockSpec((1, 2 * conv_width), const),
        ],
        out_specs=[
            pl.BlockSpec((None, tm, attn_width), row),
            pl.BlockSpec((None, tm, attn_width), row),
            pl.BlockSpec((None, tm, attn_width), row),
            pl.BlockSpec((None, tm, conv_width), row),
        ],
        out_shape=[
            jax.ShapeDtypeStruct((B, S, attn_width), F32),
            jax.ShapeDtypeStruct((B, S, attn_width), F32),
            jax.ShapeDtypeStruct((B, S, attn_width), F32),
            jax.ShapeDtypeStruct((B, S, conv_width), F32),
        ],
        compiler_params=pltpu.CompilerParams(
            dimension_semantics=("arbitrary", "arbitrary"), vmem_limit_bytes=VMEM_LIMIT_BYTES),
        name="in_proj",
    )(x, pos_lanes, freq_lanes, w_in, b_glu)


QBLK = 2 * RADIUS
KWIN = 4 * RADIUS


def _band_bias():
    rel = np.arange(KWIN)[None, :] - np.arange(QBLK)[:, None]
    tables = [np.where(np.abs(rel - i * RADIUS) <= RADIUS, 0.0, -np.inf) for i in range(3)]
    return jnp.asarray(np.stack(tables), F32)


def _window(i0, length):
    ws = jnp.clip(i0 - RADIUS, 0, length - KWIN)
    return pl.multiple_of(ws, RADIUS), (i0 - ws) // RADIUS


def _band_block(q_blk, k_win, v_win, bias, lane_lo):
    parts = []
    for lane_mask in (lane_lo, jnp.logical_not(lane_lo)):
        qh = jnp.where(lane_mask, q_blk, 0.0).astype(BF16)
        s = lax.dot_general(qh, k_win, (((1,), (1,)), ((), ())), preferred_element_type=F32) + bias
        m = jnp.max(s, axis=1, keepdims=True)
        p = jnp.exp2(s - m).astype(BF16)
        parts.append((m, jnp.dot(p, v_win, preferred_element_type=F32)))
    (m0, pv0), (m1, pv1) = parts
    m = jnp.where(lane_lo, m0, m1)
    l = jnp.where(lane_lo, pv0[:, LANES:], pv1[:, LANES:])
    acc = jnp.where(lane_lo, pv0[:, :LANES], pv1[:, :LANES])
    return m, l, acc


def _merge(*parts):
    m = functools.reduce(jnp.maximum, [p[0] for p in parts])
    scale = [jnp.exp2(p[0] - m) for p in parts]
    l = functools.reduce(jnp.add, [p[1] * e for p, e in zip(parts, scale)])
    acc = functools.reduce(jnp.add, [p[2] * e for p, e in zip(parts, scale)])
    return m, l, acc


def _attention_kernel(q_ref, k_ref, v_ref, bias_ref, o_ref,
                      k_nat, v_nat, m_nat, l_nat, acc_nat,
                      q_cls, k_cls32, v_cls32, k_cls, v_cls, k_grp, v_grp,
                      m_cls, l_cls, acc_cls, *, seq, unroll):
    n_cls = DILATIONS[1]
    cls_len = seq // n_cls
    n_grp = GROUPS // n_cls
    grp_len = cls_len // n_grp
    lane_lo = lax.broadcasted_iota(jnp.int32, (1, LANES), 1) < HEAD_DIM

    def band_pass(length, q_src, k_src, v_src, emit):
        def body(j, carry):
            i0 = pl.multiple_of(j * QBLK, QBLK)
            ws, table = _window(i0, length)
            emit(i0, *_band_block(q_src[pl.ds(i0, QBLK), :], k_src[pl.ds(ws, KWIN), :], v_src[pl.ds(ws, KWIN), :],
                                  bias_ref[table], lane_lo))
            return carry
        lax.fori_loop(0, length // QBLK, body, 0, unroll=unroll)

    k_nat[...] = k_ref[...].astype(BF16)
    v_nat[:, :LANES] = v_ref[...].astype(BF16)
    v_nat[:, LANES:] = jnp.ones((seq, LANES), BF16)

    def emit_nat(i0, m, l, acc):
        m_nat[pl.ds(i0, QBLK), :] = m
        l_nat[pl.ds(i0, QBLK), :] = l
        acc_nat[pl.ds(i0, QBLK), :] = acc

    band_pass(seq, q_ref, k_nat, v_nat, emit_nat)

    for c in range(n_cls):
        cls_rows = pl.ds(c, cls_len, stride=n_cls)
        q_cls[c] = q_ref[cls_rows, :]
        k_c = k_ref[cls_rows, :]
        v_c = v_ref[cls_rows, :]
        k_cls32[c] = k_c
        v_cls32[c] = v_c
        k_cls[c] = k_c.astype(BF16)
        v_cls[c, :, :LANES] = v_c.astype(BF16)
        v_cls[c, :, LANES:] = jnp.ones((cls_len, LANES), BF16)

        for a in range(n_grp):
            g = c * n_grp + a
            grp_rows = pl.ds(a, grp_len, stride=n_grp)
            k_grp[g] = k_cls32[c, grp_rows, :].astype(BF16)
            v_grp[g, :, :LANES] = v_cls32[c, grp_rows, :].astype(BF16)
            v_grp[g, :, LANES:] = jnp.ones((grp_len, LANES), BF16)
        for a in range(n_grp):
            g = c * n_grp + a
            for i0 in range(0, grp_len, QBLK):
                ws = min(max(i0 - RADIUS, 0), grp_len - KWIN)
                rows = pl.ds(a + n_grp * i0, QBLK, stride=n_grp)
                m, l, acc = _band_block(q_cls[c, rows, :], k_grp[g, ws:ws + KWIN, :], v_grp[g, ws:ws + KWIN, :],
                                        bias_ref[(i0 - ws) // RADIUS], lane_lo)
                m_cls[c, rows, :] = m
                l_cls[c, rows, :] = l
                acc_cls[c, rows, :] = acc

        for u0 in range(0, cls_len, QBLK):
            ws = min(max(u0 - RADIUS, 0), cls_len - KWIN)
            m, l, acc = _band_block(q_cls[c, u0:u0 + QBLK, :], k_cls[c, ws:ws + KWIN, :], v_cls[c, ws:ws + KWIN, :],
                                    bias_ref[(u0 - ws) // RADIUS], lane_lo)
            cls_blk = slice(u0, u0 + QBLK)
            nat_rows = pl.ds(c + n_cls * u0, QBLK, stride=n_cls)
            _, l, acc = _merge((m, l, acc),
                               (m_cls[c, cls_blk, :], l_cls[c, cls_blk, :], acc_cls[c, cls_blk, :]),
                               (m_nat[nat_rows, :], l_nat[nat_rows, :], acc_nat[nat_rows, :]))
            o_ref[nat_rows, :] = acc / l


def _attention(q, k, v):
    B, S, W = q.shape
    n_cls = DILATIONS[1]
    cls_len = S // n_cls
    n_grp = GROUPS // n_cls
    grp_len = cls_len // n_grp
    blk = pl.BlockSpec((None, S, LANES), lambda b, h: (b, 0, h))
    nat_f32 = pltpu.VMEM((S, LANES), F32)
    cls_f32 = pltpu.VMEM((n_cls, cls_len, LANES), F32)
    return pl.pallas_call(
        functools.partial(_attention_kernel, seq=S, unroll=ATTN_UNROLL),
        grid=(B, W // LANES),
        in_specs=[blk, blk, blk, pl.BlockSpec((3, QBLK, KWIN), lambda b, h: (0, 0, 0))],
        out_specs=blk,
        out_shape=jax.ShapeDtypeStruct((B, S, W), F32),
        scratch_shapes=[
            pltpu.VMEM((S, LANES), BF16), pltpu.VMEM((S, 2 * LANES), BF16),
            nat_f32, nat_f32, nat_f32,
            cls_f32, cls_f32, cls_f32,
            pltpu.VMEM((n_cls, cls_len, LANES), BF16), pltpu.VMEM((n_cls, cls_len, 2 * LANES), BF16),
            pltpu.VMEM((GROUPS, grp_len, LANES), BF16), pltpu.VMEM((GROUPS, grp_len, 2 * LANES), BF16),
            cls_f32, cls_f32, cls_f32,
        ],
        compiler_params=pltpu.CompilerParams(
            dimension_semantics=("arbitrary", "arbitrary"), vmem_limit_bytes=VMEM_LIMIT_BYTES),
        name="attention",
    )(q, k, v, _band_bias())


def _mix_kernel(attn_ref, up_ref, u_ref, un_ref, x_ref, w_attn_ref, w_conv_ref, cw_ref, cb_ref, cg_ref, cbeta_ref,
                g1_ref, b1_ref, o_ref, u_win, u_shift, u_act, *, alpha):
    t = pl.program_id(1)
    tm, conv_width = u_act.shape
    halo = up_ref.shape[0]

    u_win[0:halo, :] = jnp.where(t > 0, up_ref[...], 0.0)
    u_win[halo:halo + tm, :] = u_ref[...]
    u_win[halo + tm:2 * halo + tm, :] = jnp.where(t < pl.num_programs(1) - 1, un_ref[...], 0.0)

    span = u_shift.shape[1]
    for k in range(1, SUBLANES):
        u_shift[k] = u_win[k:k + span, :]

    chunk = 64
    for i in range(tm // chunk):
        acc = jnp.zeros((chunk, conv_width), F32)
        for j in range(CONV_KERNEL):
            k = (halo - CONV_PAD + j) % SUBLANES
            start = i * chunk + (halo - CONV_PAD + j) - k
            tap = u_win[start:start + chunk, :] if k == 0 else u_shift[k, start:start + chunk, :]
            acc = acc + cw_ref[j:j + 1, :] * tap
        y = _layer_norm(acc + cb_ref[...], cg_ref[...], cbeta_ref[...])
        u_act[i * chunk:(i + 1) * chunk, :] = _silu(y).astype(BF16)

    mixed = jnp.dot(attn_ref[...].astype(BF16), w_attn_ref[...], preferred_element_type=F32)
    mixed = mixed + jnp.dot(u_act[...], w_conv_ref[...], preferred_element_type=F32)
    o_ref[...] = _layer_norm(alpha * x_ref[...] + mixed, g1_ref[...], b1_ref[...])


def _mix(attn, u, x, w_attn, w_conv, conv_w, conv_b, conv_ln_g, conv_ln_b, ln_g, ln_b, alpha):
    B, S, D = x.shape
    attn_width = attn.shape[-1]
    conv_width = u.shape[-1]
    tm = ROW_TILE
    halo = 2 * SUBLANES
    assert halo >= CONV_PAD
    per_tile = tm // halo
    n_halo_blocks = S // halo
    row = lambda b, t: (b, t, 0)
    const = lambda b, t: (0, 0)
    return pl.pallas_call(
        functools.partial(_mix_kernel, alpha=alpha),
        grid=(B, S // tm),
        in_specs=[
            pl.BlockSpec((None, tm, attn_width), row),
            pl.BlockSpec((None, halo, conv_width), lambda b, t: (b, jnp.maximum(t * per_tile - 1, 0), 0)),
            pl.BlockSpec((None, tm, conv_width), row),
            pl.BlockSpec((None, halo, conv_width),
                         lambda b, t: (b, jnp.minimum((t + 1) * per_tile, n_halo_blocks - 1), 0)),
            pl.BlockSpec((None, tm, D), row),
            pl.BlockSpec((attn_width, D), const),
            pl.BlockSpec((conv_width, D), const),
            pl.BlockSpec((CONV_KERNEL, conv_width), const),
            pl.BlockSpec((1, conv_width), const),
            pl.BlockSpec((1, conv_width), const),
            pl.BlockSpec((1, conv_width), const),
            pl.BlockSpec((1, D), const),
            pl.BlockSpec((1, D), const),
        ],
        out_specs=pl.BlockSpec((None, tm, D), row),
        out_shape=jax.ShapeDtypeStruct((B, S, D), F32),
        scratch_shapes=[pltpu.VMEM((tm + 2 * halo, conv_width), F32),
                        pltpu.VMEM((SUBLANES, tm + 2 * halo - SUBLANES, conv_width), F32),
                        pltpu.VMEM((tm, conv_width), BF16)],
        compiler_params=pltpu.CompilerParams(
            dimension_semantics=("arbitrary", "arbitrary"), vmem_limit_bytes=VMEM_LIMIT_BYTES),
        name="mix",
    )(attn, u, u, u, x, w_attn, w_conv, conv_w, conv_b, conv_ln_g, conv_ln_b, ln_g, ln_b)


def _ffn_kernel(xp_ref, x_ref, xn_ref, wi_ref, wo_ref, cw_ref, cb_ref, g2_ref, b2_ref, o_ref,
                act_buf, *, alpha, chunk):
    t = pl.program_id(1)
    tm, d_ff = act_buf.shape
    x_t = x_ref[...]
    prev = jnp.where(t > 0, xp_ref[...], 0.0)
    nxt = jnp.where(t < pl.num_programs(1) - 1, xn_ref[...], 0.0)
    xb = x_t.astype(BF16)
    hb = jnp.concatenate([prev, x_t, nxt], axis=0).astype(BF16)
    rows = tm + 2 * SUBLANES
    mid = slice(SUBLANES, SUBLANES + tm)

    for c0 in range(0, d_ff, chunk):
        cols = slice(c0, c0 + chunk)
        g = jnp.dot(hb, wi_ref[:, cols], preferred_element_type=F32)
        up = jnp.dot(xb, wi_ref[:, d_ff + c0:d_ff + c0 + chunk], preferred_element_type=F32)
        gate = (cw_ref[0:1, cols] * pltpu.roll(g, 1, 0)[mid]
                + cw_ref[1:2, cols] * g[mid]
                + cw_ref[2:3, cols] * pltpu.roll(g, rows - 1, 0)[mid]
                + cb_ref[:, cols])
        act_buf[:, cols] = (_silu(gate) * up).astype(BF16)

    half = tm // 2
    for r0 in (0, half):
        ffn = jnp.dot(act_buf[r0:r0 + half, :], wo_ref[...], preferred_element_type=F32)
        o_ref[r0:r0 + half, :] = _layer_norm(alpha * x_t[r0:r0 + half] + ffn, g2_ref[...], b2_ref[...])


def _ffn(x1, w_in, w_out, conv_w, conv_b, ln_g, ln_b, alpha):
    B, S, D = x1.shape
    d_ff = w_out.shape[0]
    tm = ROW_TILE
    per_tile = tm // SUBLANES
    n_halo_blocks = S // SUBLANES
    row = lambda b, t: (b, t, 0)
    const2 = lambda b, t: (0, 0)
    return pl.pallas_call(
        functools.partial(_ffn_kernel, alpha=alpha, chunk=FFN_CHUNK),
        grid=(B, S // tm),
        in_specs=[
            pl.BlockSpec((None, SUBLANES, D), lambda b, t: (b, jnp.maximum(t * per_tile - 1, 0), 0)),
            pl.BlockSpec((None, tm, D), row),
            pl.BlockSpec((None, SUBLANES, D), lambda b, t: (b, jnp.minimum((t + 1) * per_tile, n_halo_blocks - 1), 0)),
            pl.BlockSpec((D, 2 * d_ff), const2),
            pl.BlockSpec((d_ff, D), const2),
            pl.BlockSpec((FFN_CONV_KERNEL, d_ff), const2),
            pl.BlockSpec((1, d_ff), const2),
            pl.BlockSpec((1, D), const2),
            pl.BlockSpec((1, D), const2),
        ],
        out_specs=pl.BlockSpec((None, tm, D), row),
        out_shape=jax.ShapeDtypeStruct((B, S, D), F32),
        scratch_shapes=[pltpu.VMEM((tm, d_ff), BF16)],
        compiler_params=pltpu.CompilerParams(
            dimension_semantics=("arbitrary", "arbitrary"), vmem_limit_bytes=VMEM_LIMIT_BYTES),
        name="ffn",
    )(x1, x1, x1, w_in, w_out, conv_w, conv_b, ln_g, ln_b)


def kernel(x, positions, w_in, b_glu, conv_w, conv_b, conv_ln_g, conv_ln_b, w_out, ln1_g, ln1_b,
           w_ffn_in, ffn_conv_w, ffn_conv_b, w_ffn_out, ln2_g, ln2_b):
    B, S, D = x.shape
    depth = w_in.shape[0]
    conv_width = conv_w.shape[-1]
    attn_width = w_out.shape[1] - conv_width
    d_ff = w_ffn_out.shape[1]
    alpha = (2.0 * depth) ** 0.25
    assert S % (GROUPS * QBLK) == 0 and S // GROUPS >= KWIN and S % ROW_TILE == 0
    assert attn_width % LANES == 0 and d_ff % FFN_CHUNK == 0 and w_in.shape[-1] == 3 * attn_width + 2 * conv_width

    inv_freq = ROPE_THETA ** (-jnp.arange(ROPE_HALF, dtype=F32) / ROPE_HALF)
    freq_lanes = inv_freq[jnp.arange(LANES) % ROPE_HALF][None, :]
    pos_lanes = jnp.broadcast_to(positions.astype(F32)[:, :, None], (B, S, LANES))

    for l in range(depth):
        q, k, v, u = _in_proj(x, pos_lanes, freq_lanes, w_in[l].astype(BF16), b_glu[l][None, :],
                              attn_width, conv_width)
        attn = _attention(q, k, v)
        w_o = w_out[l].astype(BF16)
        x = _mix(attn, u, x, w_o[:attn_width], w_o[attn_width:], conv_w[l], conv_b[l][None, :],
                 conv_ln_g[l][None, :], conv_ln_b[l][None, :], ln1_g[l][None, :], ln1_b[l][None, :], alpha)
        x = _ffn(x, w_ffn_in[l].astype(BF16), w_ffn_out[l].astype(BF16), ffn_conv_w[l], ffn_conv_b[l][None, :],
                 ln2_g[l][None, :], ln2_b[l][None, :], alpha)
    return x
```

```python
import functools
import math

import jax
import jax.numpy as jnp
import numpy as np
from jax import lax
from jax.experimental import pallas as pl
from jax.experimental.pallas import tpu as pltpu

F32 = jnp.float32
BF16 = jnp.bfloat16

HEAD_DIM = 64
ROPE_DIM = HEAD_DIM // 4
ROPE_HALF = ROPE_DIM // 2
ROPE_THETA = 500000.0
CONV_KERNEL = 31
CONV_PAD = (CONV_KERNEL - 1) // 2
FFN_CONV_KERNEL = 3
DILATIONS = (1, 4, 16)
RADIUS = 64
LN_EPS = 1e-5

LANES = 128
SUBLANES = 8
GROUPS = DILATIONS[-1]
VMEM_LIMIT_BYTES = 56 * 1024 * 1024

ROW_TILE = 512
FFN_CHUNK = 256
ATTN_UNROLL = 32


def _layer_norm(y, g, b):
    mu = jnp.mean(y, axis=-1, keepdims=True)
    yc = y - mu
    var = jnp.mean(yc * yc, axis=-1, keepdims=True)
    return yc * lax.rsqrt(var + LN_EPS) * g + b


def _silu(y):
    return y * (1.0 / (1.0 + jnp.exp(-y)))


def _in_proj_kernel(x_ref, pos_ref, freq_ref, w_ref, bglu_ref, q_ref, k_ref, v_ref, u_ref, *, attn_width, conv_width):
    xb = x_ref[...].astype(BF16)
    ang = pos_ref[...] * freq_ref[...]
    cos = jnp.cos(ang)
    sin = jnp.sin(ang)
    d = lax.broadcasted_iota(jnp.int32, (1, LANES), 1) & (HEAD_DIM - 1)
    c_t = jnp.where(d < ROPE_DIM, cos, 1.0)
    a_t = jnp.where(d < ROPE_HALF, -sin, 0.0)
    b_t = jnp.where((d >= ROPE_HALF) & (d < ROPE_DIM), sin, 0.0)

    def rope(t):
        return t * c_t + pltpu.roll(t, LANES - ROPE_HALF, 1) * a_t + pltpu.roll(t, ROPE_HALF, 1) * b_t

    scale = HEAD_DIM ** -0.5 * math.log2(math.e)
    q = jnp.dot(xb, w_ref[:, 0:attn_width], preferred_element_type=F32)
    for i in range(attn_width // LANES):
        q_ref[:, i * LANES:(i + 1) * LANES] = rope(q[:, i * LANES:(i + 1) * LANES]) * scale
    k = jnp.dot(xb, w_ref[:, attn_width:2 * attn_width], preferred_element_type=F32)
    for i in range(attn_width // LANES):
        k_ref[:, i * LANES:(i + 1) * LANES] = rope(k[:, i * LANES:(i + 1) * LANES])
    v_ref[...] = jnp.dot(xb, w_ref[:, 2 * attn_width:3 * attn_width], preferred_element_type=F32)
    glu = jnp.dot(xb, w_ref[:, 3 * attn_width:3 * attn_width + 2 * conv_width], preferred_element_type=F32)
    glu = glu + bglu_ref[...]
    c_val = glu[:, 0:conv_width]
    c_gate = glu[:, conv_width:2 * conv_width]
    u_ref[...] = c_val * (1.0 / (1.0 + jnp.exp(-c_gate)))


def _in_proj(x, pos_lanes, freq_lanes, w_in, b_glu, attn_width, conv_width):
    B, S, D = x.shape
    n_in = w_in.shape[1]
    tm = ROW_TILE
    row = lambda b, t: (b, t, 0)
    const = lambda b, t: (0, 0)
    return pl.pallas_call(
        functools.partial(_in_proj_kernel, attn_width=attn_width, conv_width=conv_width),
        grid=(B, S // tm),
        in_specs=[
            pl.BlockSpec((None, tm, D), row),
            pl.BlockSpec((None, tm, LANES), row),
            pl.BlockSpec((1, LANES), const),
            pl.BlockSpec((D, n_in), const),
            pl.BlockSpec((1, 2 * conv_width), const),
        ],
        out_specs=[
            pl.BlockSpec((None, tm, attn_width), row),
            pl.BlockSpec((None, tm, attn_width), row),
            pl.BlockSpec((None, tm, attn_width), row),
            pl.BlockSpec((None, tm, conv_width), row),
        ],
        out_shape=[
            jax.ShapeDtypeStruct((B, S, attn_width), F32),
            jax.ShapeDtypeStruct((B, S, attn_width), F32),
            jax.ShapeDtypeStruct((B, S, attn_width), F32),
            jax.ShapeDtypeStruct((B, S, conv_width), F32),
        ],
        compiler_params=pltpu.CompilerParams(
            dimension_semantics=("arbitrary", "arbitrary"), vmem_limit_bytes=VMEM_LIMIT_BYTES),
        name="in_proj",
    )(x, pos_lanes, freq_lanes, w_in, b_glu)


QBLK = 2 * RADIUS
KWIN = 4 * RADIUS


def _band_bias():
    rel = np.arange(KWIN)[None, :] - np.arange(QBLK)[:, None]
    tables = [np.where(np.abs(rel - i * RADIUS) <= RADIUS, 0.0, -np.inf) for i in range(3)]
    return jnp.asarray(np.stack(tables), F32)


def _window(i0, length):
    ws = jnp.clip(i0 - RADIUS, 0, length - KWIN)
    return pl.multiple_of(ws, RADIUS), (i0 - ws) // RADIUS


def _band_block(q_blk, k_win, v_win, bias, lane_lo):
    parts = []
    for lane_mask in (lane_lo, jnp.logical_not(lane_lo)):
        qh = jnp.where(lane_mask, q_blk, 0.0).astype(BF16)
        s = lax.dot_general(qh, k_win, (((1,), (1,)), ((), ())), preferred_element_type=F32) + bias
        m = jnp.max(s, axis=1, keepdims=True)
        p = jnp.exp2(s - m).astype(BF16)
        parts.append((m, jnp.dot(p, v_win, preferred_element_type=F32)))
    (m0, pv0), (m1, pv1) = parts
    m = jnp.where(lane_lo, m0, m1)
    l = jnp.where(lane_lo, pv0[:, LANES:], pv1[:, LANES:])
    acc = jnp.where(lane_lo, pv0[:, :LANES], pv1[:, :LANES])
    return m, l, acc


def _merge(*parts):
    m = functools.reduce(jnp.maximum, [p[0] for p in parts])
    scale = [jnp.exp2(p[0] - m) for p in parts]
    l = functools.reduce(jnp.add, [p[1] * e for p, e in zip(parts, scale)])
    acc = functools.reduce(jnp.add, [p[2] * e for p, e in zip(parts, scale)])
    return m, l, acc


def _attention_kernel(q_ref, k_ref, v_ref, bias_ref, o_ref,
                      k_nat, v_nat, m_nat, l_nat, acc_nat,
                      q_cls, k_cls32, v_cls32, k_cls, v_cls, k_grp, v_grp,
                      m_cls, l_cls, acc_cls, *, seq, unroll):
    n_cls = DILATIONS[1]
    cls_len = seq // n_cls
    n_grp = GROUPS // n_cls
    grp_len = cls_len // n_grp
    lane_lo = lax.broadcasted_iota(jnp.int32, (1, LANES), 1) < HEAD_DIM

    def band_pass(length, q_src, k_src, v_src, emit):
        def body(j, carry):
            i0 = pl.multiple_of(j * QBLK, QBLK)
            ws, table = _window(i0, length)
            emit(i0, *_band_block(q_src[pl.ds(i0, QBLK), :], k_src[pl.ds(ws, KWIN), :], v_src[pl.ds(ws, KWIN), :],
                                  bias_ref[table], lane_lo))
            return carry
        lax.fori_loop(0, length // QBLK, body, 0, unroll=unroll)

    k_nat[...] = k_ref[...].astype(BF16)
    v_nat[:, :LANES] = v_ref[...].astype(BF16)
    v_nat[:, LANES:] = jnp.ones((seq, LANES), BF16)

    def emit_nat(i0, m, l, acc):
        m_nat[pl.ds(i0, QBLK), :] = m
        l_nat[pl.ds(i0, QBLK), :] = l
        acc_nat[pl.ds(i0, QBLK), :] = acc

    band_pass(seq, q_ref, k_nat, v_nat, emit_nat)

    for c in range(n_cls):
        cls_rows = pl.ds(c, cls_len, stride=n_cls)
        q_cls[c] = q_ref[cls_rows, :]
        k_c = k_ref[cls_rows, :]
        v_c = v_ref[cls_rows, :]
        k_cls32[c] = k_c
        v_cls32[c] = v_c
        k_cls[c] = k_c.astype(BF16)
        v_cls[c, :, :LANES] = v_c.astype(BF16)
        v_cls[c, :, LANES:] = jnp.ones((cls_len, LANES), BF16)

        for a in range(n_grp):
            g = c * n_grp + a
            grp_rows = pl.ds(a, grp_len, stride=n_grp)
            k_grp[g] = k_cls32[c, grp_rows, :].astype(BF16)
            v_grp[g, :, :LANES] = v_cls32[c, grp_rows, :].astype(BF16)
            v_grp[g, :, LANES:] = jnp.ones((grp_len, LANES), BF16)
        for a in range(n_grp):
            g = c * n_grp + a
            for i0 in range(0, grp_len, QBLK):
                ws = min(max(i0 - RADIUS, 0), grp_len - KWIN)
                rows = pl.ds(a + n_grp * i0, QBLK, stride=n_grp)
                m, l, acc = _band_block(q_cls[c, rows, :], k_grp[g, ws:ws + KWIN, :], v_grp[g, ws:ws + KWIN, :],
                                        bias_ref[(i0 - ws) // RADIUS], lane_lo)
                m_cls[c, rows, :] = m
                l_cls[c, rows, :] = l
                acc_cls[c, rows, :] = acc

        for u0 in range(0, cls_len, QBLK):
            ws = min(max(u0 - RADIUS, 0), cls_len - KWIN)
            m, l, acc = _band_block(q_cls[c, u0:u0 + QBLK, :], k_cls[c, ws:ws + KWIN, :], v_cls[c, ws:ws + KWIN, :],
                                    bias_ref[(u0 - ws) // RADIUS], lane_lo)
            cls_blk = slice(u0, u0 + QBLK)
            nat_rows = pl.ds(c + n_cls * u0, QBLK, stride=n_cls)
            _, l, acc = _merge((m, l, acc),
                               (m_cls[c, cls_blk, :], l_cls[c, cls_blk, :], acc_cls[c, cls_blk, :]),
                               (m_nat[nat_rows, :], l_nat[nat_rows, :], acc_nat[nat_rows, :]))
            o_ref[nat_rows, :] = acc / l


def _attention(q, k, v):
    B, S, W = q.shape
    n_cls = DILATIONS[1]
    cls_len = S // n_cls
    n_grp = GROUPS // n_cls
    grp_len = cls_len // n_grp
    blk = pl.BlockSpec((None, S, LANES), lambda b, h: (b, 0, h))
    nat_f32 = pltpu.VMEM((S, LANES), F32)
    cls_f32 = pltpu.VMEM((n_cls, cls_len, LANES), F32)
    return pl.pallas_call(
        functools.partial(_attention_kernel, seq=S, unroll=ATTN_UNROLL),
        grid=(B, W // LANES),
        in_specs=[blk, blk, blk, pl.BlockSpec((3, QBLK, KWIN), lambda b, h: (0, 0, 0))],
        out_specs=blk,
        out_shape=jax.ShapeDtypeStruct((B, S, W), F32),
        scratch_shapes=[
            pltpu.VMEM((S, LANES), BF16), pltpu.VMEM((S, 2 * LANES), BF16),
            nat_f32, nat_f32, nat_f32,
            cls_f32, cls_f32, cls_f32,
            pltpu.VMEM((n_cls, cls_len, LANES), BF16), pltpu.VMEM((n_cls, cls_len, 2 * LANES), BF16),
            pltpu.VMEM((GROUPS, grp_len, LANES), BF16), pltpu.VMEM((GROUPS, grp_len, 2 * LANES), BF16),
            cls_f32, cls_f32, cls_f32,
        ],
        compiler_params=pltpu.CompilerParams(
            dimension_semantics=("arbitrary", "arbitrary"), vmem_limit_bytes=VMEM_LIMIT_BYTES),
        name="attention",
    )(q, k, v, _band_bias())


def _mix_kernel(attn_ref, up_ref, u_ref, un_ref, x_ref, w_attn_ref, w_conv_ref, cw_ref, cb_ref, cg_ref, cbeta_ref,
                g1_ref, b1_ref, o_ref, u_win, u_shift, u_act, *, alpha):
    t = pl.program_id(1)
    tm, conv_width = u_act.shape
    halo = up_ref.shape[0]

    u_win[0:halo, :] = jnp.where(t > 0, up_ref[...], 0.0)
    u_win[halo:halo + tm, :] = u_ref[...]
    u_win[halo + tm:2 * halo + tm, :] = jnp.where(t < pl.num_programs(1) - 1, un_ref[...], 0.0)

    span = u_shift.shape[1]
    for k in range(1, SUBLANES):
        u_shift[k] = u_win[k:k + span, :]

    chunk = 64
    for i in range(tm // chunk):
        acc = jnp.zeros((chunk, conv_width), F32)
        for j in range(CONV_KERNEL):
            k = (halo - CONV_PAD + j) % SUBLANES
            start = i * chunk + (halo - CONV_PAD + j) - k
            tap = u_win[start:start + chunk, :] if k == 0 else u_shift[k, start:start + chunk, :]
            acc = acc + cw_ref[j:j + 1, :] * tap
        y = _layer_norm(acc + cb_ref[...], cg_ref[...], cbeta_ref[...])
        u_act[i * chunk:(i + 1) * chunk, :] = _silu(y).astype(BF16)

    mixed = jnp.dot(attn_ref[...].astype(BF16), w_attn_ref[...], preferred_element_type=F32)
    mixed = mixed + jnp.dot(u_act[...], w_conv_ref[...], preferred_element_type=F32)
    o_ref[...] = _layer_norm(alpha * x_ref[...] + mixed, g1_ref[...], b1_ref[...])


def _mix(attn, u, x, w_attn, w_conv, conv_w, conv_b, conv_ln_g, conv_ln_b, ln_g, ln_b, alpha):
    B, S, D = x.shape
    attn_width = attn.shape[-1]
    conv_width = u.shape[-1]
    tm = ROW_TILE
    halo = 2 * SUBLANES
    assert halo >= CONV_PAD
    per_tile = tm // halo
    n_halo_blocks = S // halo
    row = lambda b, t: (b, t, 0)
    const = lambda b, t: (0, 0)
    return pl.pallas_call(
        functools.partial(_mix_kernel, alpha=alpha),
        grid=(B, S // tm),
        in_specs=[
            pl.BlockSpec((None, tm, attn_width), row),
            pl.BlockSpec((None, halo, conv_width), lambda b, t: (b, jnp.maximum(t * per_tile - 1, 0), 0)),
            pl.BlockSpec((None, tm, conv_width), row),
            pl.BlockSpec((None, halo, conv_width),
                         lambda b, t: (b, jnp.minimum((t + 1) * per_tile, n_halo_blocks - 1), 0)),
            pl.BlockSpec((None, tm, D), row),
            pl.BlockSpec((attn_width, D), const),
            pl.BlockSpec((conv_width, D), const),
            pl.BlockSpec((CONV_KERNEL, conv_width), const),
            pl.BlockSpec((1, conv_width), const),
            pl.BlockSpec((1, conv_width), const),
            pl.BlockSpec((1, conv_width), const),
            pl.BlockSpec((1, D), const),
            pl.BlockSpec((1, D), const),
        ],
        out_specs=pl.BlockSpec((None, tm, D), row),
        out_shape=jax.ShapeDtypeStruct((B, S, D), F32),
        scratch_shapes=[pltpu.VMEM((tm + 2 * halo, conv_width), F32),
                        pltpu.VMEM((SUBLANES, tm + 2 * halo - SUBLANES, conv_width), F32),
                        pltpu.VMEM((tm, conv_width), BF16)],
        compiler_params=pltpu.CompilerParams(
            dimension_semantics=("arbitrary", "arbitrary"), vmem_limit_bytes=VMEM_LIMIT_BYTES),
        name="mix",
    )(attn, u, u, u, x, w_attn, w_conv, conv_w, conv_b, conv_ln_g, conv_ln_b, ln_g, ln_b)


def _ffn_kernel(xp_ref, x_ref, xn_ref, wi_ref, wo_ref, cw_ref, cb_ref, g2_ref, b2_ref, o_ref,
                act_buf, *, alpha, chunk):
    t = pl.program_id(1)
    tm, d_ff = act_buf.shape
    x_t = x_ref[...]
    prev = jnp.where(t > 0, xp_ref[...], 0.0)
    nxt = jnp.where(t < pl.num_programs(1) - 1, xn_ref[...], 0.0)
    xb = x_t.astype(BF16)
    hb = jnp.concatenate([prev, x_t, nxt], axis=0).astype(BF16)
    rows = tm + 2 * SUBLANES
    mid = slice(SUBLANES, SUBLANES + tm)

    for c0 in range(0, d_ff, chunk):
        cols = slice(c0, c0 + chunk)
        g = jnp.dot(hb, wi_ref[:, cols], preferred_element_type=F32)
        up = jnp.dot(xb, wi_ref[:, d_ff + c0:d_ff + c0 + chunk], preferred_element_type=F32)
        gate = (cw_ref[0:1, cols] * pltpu.roll(g, 1, 0)[mid]
                + cw_ref[1:2, cols] * g[mid]
                + cw_ref[2:3, cols] * pltpu.roll(g, rows - 1, 0)[mid]
                + cb_ref[:, cols])
        act_buf[:, cols] = (_silu(gate) * up).astype(BF16)

    half = tm // 2
    for r0 in (0, half):
        ffn = jnp.dot(act_buf[r0:r0 + half, :], wo_ref[...], preferred_element_type=F32)
        o_ref[r0:r0 + half, :] = _layer_norm(alpha * x_t[r0:r0 + half] + ffn, g2_ref[...], b2_ref[...])


def _ffn(x1, w_in, w_out, conv_w, conv_b, ln_g, ln_b, alpha):
    B, S, D = x1.shape
    d_ff = w_out.shape[0]
    tm = ROW_TILE
    per_tile = tm // SUBLANES
    n_halo_blocks = S // SUBLANES
    row = lambda b, t: (b, t, 0)
    const2 = lambda b, t: (0, 0)
    return pl.pallas_call(
        functools.partial(_ffn_kernel, alpha=alpha, chunk=FFN_CHUNK),
        grid=(B, S // tm),
        in_specs=[
            pl.BlockSpec((None, SUBLANES, D), lambda b, t: (b, jnp.maximum(t * per_tile - 1, 0), 0)),
            pl.BlockSpec((None, tm, D), row),
            pl.BlockSpec((None, SUBLANES, D), lambda b, t: (b, jnp.minimum((t + 1) * per_tile, n_halo_blocks - 1), 0)),
            pl.BlockSpec((D, 2 * d_ff), const2),
            pl.BlockSpec((d_ff, D), const2),
            pl.BlockSpec((FFN_CONV_KERNEL, d_ff), const2),
            pl.BlockSpec((1, d_ff), const2),
            pl.BlockSpec((1, D), const2),
            pl.BlockSpec((1, D), const2),
        ],
        out_specs=pl.BlockSpec((None, tm, D), row),
        out_shape=jax.ShapeDtypeStruct((B, S, D), F32),
        scratch_shapes=[pltpu.VMEM((tm, d_ff), BF16)],
        compiler_params=pltpu.CompilerParams(
            dimension_semantics=("arbitrary", "arbitrary"), vmem_limit_bytes=VMEM_LIMIT_BYTES),
        name="ffn",
    )(x1, x1, x1, w_in, w_out, conv_w, conv_b, ln_g, ln_b)


def kernel(x, positions, w_in, b_glu, conv_w, conv_b, conv_ln_g, conv_ln_b, w_out, ln1_g, ln1_b,
           w_ffn_in, ffn_conv_w, ffn_conv_b, w_ffn_out, ln2_g, ln2_b):
    B, S, D = x.shape
    depth = w_in.shape[0]
    conv_width = conv_w.shape[-1]
    attn_width = w_out.shape[1] - conv_width
    d_ff = w_ffn_out.shape[1]
    alpha = (2.0 * depth) ** 0.25
    assert S % (GROUPS * QBLK) == 0 and S // GROUPS >= KWIN and S % ROW_TILE == 0
    assert attn_width % LANES == 0 and d_ff % FFN_CHUNK == 0 and w_in.shape[-1] == 3 * attn_width + 2 * conv_width

    inv_freq = ROPE_THETA ** (-jnp.arange(ROPE_HALF, dtype=F32) / ROPE_HALF)
    freq_lanes = inv_freq[jnp.arange(LANES) % ROPE_HALF][None, :]
    pos_lanes = jnp.broadcast_to(positions.astype(F32)[:, :, None], (B, S, LANES))

    for l in range(depth):
        q, k, v, u = _in_proj(x, pos_lanes, freq_lanes, w_in[l].astype(BF16), b_glu[l][None, :],
                              attn_width, conv_width)
        attn = _attention(q, k, v)
        w_o = w_out[l].astype(BF16)
        x = _mix(attn, u, x, w_o[:attn_width], w_o[attn_width:], conv_w[l], conv_b[l][None, :],
                 conv_ln_g[l][None, :], conv_ln_b[l][None, :], ln1_g[l][None, :], ln1_b[l][None, :], alpha)
        x = _ffn(x, w_ffn_in[l].astype(BF16), w_ffn_out[l].astype(BF16), ffn_conv_w[l], ffn_conv_b[l][None, :],
                 ln2_g[l][None, :], ln2_b[l][None, :], alpha)
    return x
```

```python
import functools
import math

import jax
import jax.numpy as jnp
import numpy as np
from jax import lax
from jax.experimental import pallas as pl
from jax.experimental.pallas import tpu as pltpu

F32 = jnp.float32
BF16 = jnp.bfloat16

HEAD_DIM = 64
ROPE_DIM = HEAD_DIM // 4
ROPE_HALF = ROPE_DIM // 2
ROPE_THETA = 500000.0
CONV_KERNEL = 31
CONV_PAD = (CONV_KERNEL - 1) // 2
FFN_CONV_KERNEL = 3
DILATIONS = (1, 4, 16)
RADIUS = 64
LN_EPS = 1e-5

LANES = 128
SUBLANES = 8
GROUPS = DILATIONS[-1]
VMEM_LIMIT_BYTES = 56 * 1024 * 1024

ROW_TILE = 512
FFN_CHUNK = 256
ATTN_UNROLL = 32


def _layer_norm(y, g, b):
    mu = jnp.mean(y, axis=-1, keepdims=True)
    yc = y - mu
    var = jnp.mean(yc * yc, axis=-1, keepdims=True)
    return yc * lax.rsqrt(var + LN_EPS) * g + b


def _silu(y):
    return y * (1.0 / (1.0 + jnp.exp(-y)))


def _in_proj_kernel(x_ref, pos_ref, freq_ref, w_ref, bglu_ref, q_ref, k_ref, v_ref, u_ref, *, attn_width, conv_width):
    xb = x_ref[...].astype(BF16)
    ang = pos_ref[...] * freq_ref[...]
    cos = jnp.cos(ang)
    sin = jnp.sin(ang)
    d = lax.broadcasted_iota(jnp.int32, (1, LANES), 1) & (HEAD_DIM - 1)
    c_t = jnp.where(d < ROPE_DIM, cos, 1.0)
    a_t = jnp.where(d < ROPE_HALF, -sin, 0.0)
    b_t = jnp.where((d >= ROPE_HALF) & (d < ROPE_DIM), sin, 0.0)

    def rope(t):
        return t * c_t + pltpu.roll(t, LANES - ROPE_HALF, 1) * a_t + pltpu.roll(t, ROPE_HALF, 1) * b_t

    scale = HEAD_DIM ** -0.5 * math.log2(math.e)
    q = jnp.dot(xb, w_ref[:, 0:attn_width], preferred_element_type=F32)
    for i in range(attn_width // LANES):
        q_ref[i] = rope(q[:, i * LANES:(i + 1) * LANES]) * scale
    k = jnp.dot(xb, w_ref[:, attn_width:2 * attn_width], preferred_element_type=F32)
    for i in range(attn_width // LANES):
        k_ref[i] = rope(k[:, i * LANES:(i + 1) * LANES])
    v = jnp.dot(xb, w_ref[:, 2 * attn_width:3 * attn_width], preferred_element_type=F32)
    for i in range(attn_width // LANES):
        v_ref[i] = v[:, i * LANES:(i + 1) * LANES]
    glu = jnp.dot(xb, w_ref[:, 3 * attn_width:3 * attn_width + 2 * conv_width], preferred_element_type=F32)
    glu = glu + bglu_ref[...]
    c_val = glu[:, 0:conv_width]
    c_gate = glu[:, conv_width:2 * conv_width]
    u_ref[...] = c_val * (1.0 / (1.0 + jnp.exp(-c_gate)))


def _in_proj(x, pos_lanes, freq_lanes, w_in, b_glu, attn_width, conv_width):
    B, S, D = x.shape
    n_in = w_in.shape[1]
    n_pairs = attn_width // LANES
    tm = ROW_TILE
    row = lambda b, t: (b, t, 0)
    pairs = lambda b, t: (b, 0, t, 0)
    const = lambda b, t: (0, 0)
    return pl.pallas_call(
        functools.partial(_in_proj_kernel, attn_width=attn_width, conv_width=conv_width),
        grid=(B, S // tm),
        in_specs=[
            pl.BlockSpec((None, tm, D), row),
            pl.BlockSpec((None, tm, LANES), row),
            pl.BlockSpec((1, LANES), const),
            pl.BlockSpec((D, n_in), const),
            pl.BlockSpec((1, 2 * conv_width), const),
        ],
        out_specs=[
            pl.BlockSpec((None, n_pairs, tm, LANES), pairs),
            pl.BlockSpec((None, n_pairs, tm, LANES), pairs),
            pl.BlockSpec((None, n_pairs, tm, LANES), pairs),
            pl.BlockSpec((None, tm, conv_width), row),
        ],
        out_shape=[
            jax.ShapeDtypeStruct((B, n_pairs, S, LANES), F32),
            jax.ShapeDtypeStruct((B, n_pairs, S, LANES), F32),
            jax.ShapeDtypeStruct((B, n_pairs, S, LANES), F32),
            jax.ShapeDtypeStruct((B, S, conv_width), F32),
        ],
        compiler_params=pltpu.CompilerParams(
            dimension_semantics=("arbitrary", "arbitrary"), vmem_limit_bytes=VMEM_LIMIT_BYTES),
        name="in_proj",
    )(x, pos_lanes, freq_lanes, w_in, b_glu)


QBLK = 2 * RADIUS
KWIN = 4 * RADIUS


def _band_bias():
    rel = np.arange(KWIN)[None, :] - np.arange(QBLK)[:, None]
    tables = [np.where(np.abs(rel - i * RADIUS) <= RADIUS, 0.0, -np.inf) for i in range(3)]
    return jnp.asarray(np.stack(tables), F32)


def _window(i0, length):
    ws = jnp.clip(i0 - RADIUS, 0, length - KWIN)
    return pl.multiple_of(ws, RADIUS), (i0 - ws) // RADIUS


def _band_block(q_blk, k_win, v_win, bias, lane_lo):
    parts = []
    for lane_mask in (lane_lo, jnp.logical_not(lane_lo)):
        qh = jnp.where(lane_mask, q_blk, 0.0).astype(BF16)
        s = lax.dot_general(qh, k_win, (((1,), (1,)), ((), ())), preferred_element_type=F32) + bias
        m = jnp.max(s, axis=1, keepdims=True)
        p = jnp.exp2(s - m).astype(BF16)
        parts.append((m, jnp.dot(p, v_win, preferred_element_type=F32)))
    (m0, pv0), (m1, pv1) = parts
    m = jnp.where(lane_lo, m0, m1)
    l = jnp.where(lane_lo, pv0[:, LANES:], pv1[:, LANES:])
    acc = jnp.where(lane_lo, pv0[:, :LANES], pv1[:, :LANES])
    return m, l, acc


def _merge(*parts):
    m = functools.reduce(jnp.maximum, [p[0] for p in parts])
    scale = [jnp.exp2(p[0] - m) for p in parts]
    l = functools.reduce(jnp.add, [p[1] * e for p, e in zip(parts, scale)])
    acc = functools.reduce(jnp.add, [p[2] * e for p, e in zip(parts, scale)])
    return m, l, acc


def _attention_kernel(q_ref, k_ref, v_ref, bias_ref, o_ref,
                      k_nat, v_nat, m_nat, l_nat, acc_nat,
                      q_cls, k_cls32, v_cls32, k_cls, v_cls, k_grp, v_grp,
                      m_cls, l_cls, acc_cls, *, seq, unroll):
    n_cls = DILATIONS[1]
    cls_len = seq // n_cls
    n_grp = GROUPS // n_cls
    grp_len = cls_len // n_grp
    lane_lo = lax.broadcasted_iota(jnp.int32, (1, LANES), 1) < HEAD_DIM

    def band_pass(length, q_src, k_src, v_src, emit):
        def body(j, carry):
            i0 = pl.multiple_of(j * QBLK, QBLK)
            ws, table = _window(i0, length)
            emit(i0, *_band_block(q_src[pl.ds(i0, QBLK), :], k_src[pl.ds(ws, KWIN), :], v_src[pl.ds(ws, KWIN), :],
                                  bias_ref[table], lane_lo))
            return carry
        lax.fori_loop(0, length // QBLK, body, 0, unroll=unroll)

    k_nat[...] = k_ref[...].astype(BF16)
    v_nat[:, :LANES] = v_ref[...].astype(BF16)
    v_nat[:, LANES:] = jnp.ones((seq, LANES), BF16)

    def emit_nat(i0, m, l, acc):
        m_nat[pl.ds(i0, QBLK), :] = m
        l_nat[pl.ds(i0, QBLK), :] = l
        acc_nat[pl.ds(i0, QBLK), :] = acc

    band_pass(seq, q_ref, k_nat, v_nat, emit_nat)

    for c in range(n_cls):
        cls_rows = pl.ds(c, cls_len, stride=n_cls)
        q_cls[c] = q_ref[cls_rows, :]
        k_c = k_ref[cls_rows, :]
        v_c = v_ref[cls_rows, :]
        k_cls32[c] = k_c
        v_cls32[c] = v_c
        k_cls[c] = k_c.astype(BF16)
        v_cls[c, :, :LANES] = v_c.astype(BF16)
        v_cls[c, :, LANES:] = jnp.ones((cls_len, LANES), BF16)

        for a in range(n_grp):
            g = c * n_grp + a
            grp_rows = pl.ds(a, grp_len, stride=n_grp)
            k_grp[g] = k_cls32[c, grp_rows, :].astype(BF16)
            v_grp[g, :, :LANES] = v_cls32[c, grp_rows, :].astype(BF16)
            v_grp[g, :, LANES:] = jnp.ones((grp_len, LANES), BF16)
        for a in range(n_grp):
            g = c * n_grp + a
            for i0 in range(0, grp_len, QBLK):
                ws = min(max(i0 - RADIUS, 0), grp_len - KWIN)
                rows = pl.ds(a + n_grp * i0, QBLK, stride=n_grp)
                m, l, acc = _band_block(q_cls[c, rows, :], k_grp[g, ws:ws + KWIN, :], v_grp[g, ws:ws + KWIN, :],
                                        bias_ref[(i0 - ws) // RADIUS], lane_lo)
                m_cls[c, rows, :] = m
                l_cls[c, rows, :] = l
                acc_cls[c, rows, :] = acc

        for u0 in range(0, cls_len, QBLK):
            ws = min(max(u0 - RADIUS, 0), cls_len - KWIN)
            m, l, acc = _band_block(q_cls[c, u0:u0 + QBLK, :], k_cls[c, ws:ws + KWIN, :], v_cls[c, ws:ws + KWIN, :],
                                    bias_ref[(u0 - ws) // RADIUS], lane_lo)
            cls_blk = slice(u0, u0 + QBLK)
            nat_rows = pl.ds(c + n_cls * u0, QBLK, stride=n_cls)
            _, l, acc = _merge((m, l, acc),
                               (m_cls[c, cls_blk, :], l_cls[c, cls_blk, :], acc_cls[c, cls_blk, :]),
                               (m_nat[nat_rows, :], l_nat[nat_rows, :], acc_nat[nat_rows, :]))
            o_ref[nat_rows, :] = acc / l


def _attention(q, k, v):
    B, n_pairs, S, _ = q.shape
    n_cls = DILATIONS[1]
    cls_len = S // n_cls
    n_grp = GROUPS // n_cls
    grp_len = cls_len // n_grp
    blk = pl.BlockSpec((None, None, S, LANES), lambda b, h: (b, h, 0, 0))
    nat_f32 = pltpu.VMEM((S, LANES), F32)
    cls_f32 = pltpu.VMEM((n_cls, cls_len, LANES), F32)
    return pl.pallas_call(
        functools.partial(_attention_kernel, seq=S, unroll=ATTN_UNROLL),
        grid=(B, n_pairs),
        in_specs=[blk, blk, blk, pl.BlockSpec((3, QBLK, KWIN), lambda b, h: (0, 0, 0))],
        out_specs=blk,
        out_shape=jax.ShapeDtypeStruct((B, n_pairs, S, LANES), F32),
        scratch_shapes=[
            pltpu.VMEM((S, LANES), BF16), pltpu.VMEM((S, 2 * LANES), BF16),
            nat_f32, nat_f32, nat_f32,
            cls_f32, cls_f32, cls_f32,
            pltpu.VMEM((n_cls, cls_len, LANES), BF16), pltpu.VMEM((n_cls, cls_len, 2 * LANES), BF16),
            pltpu.VMEM((GROUPS, grp_len, LANES), BF16), pltpu.VMEM((GROUPS, grp_len, 2 * LANES), BF16),
            cls_f32, cls_f32, cls_f32,
        ],
        compiler_params=pltpu.CompilerParams(
            dimension_semantics=("arbitrary", "arbitrary"), vmem_limit_bytes=VMEM_LIMIT_BYTES),
        name="attention",
    )(q, k, v, _band_bias())


def _mix_kernel(attn_ref, up_ref, u_ref, un_ref, x_ref, w_attn_ref, w_conv_ref, cw_ref, cb_ref, cg_ref, cbeta_ref,
                g1_ref, b1_ref, o_ref, u_win, u_shift, u_act, *, alpha):
    t = pl.program_id(1)
    tm, conv_width = u_act.shape
    halo = up_ref.shape[0]

    u_win[0:halo, :] = jnp.where(t > 0, up_ref[...], 0.0)
    u_win[halo:halo + tm, :] = u_ref[...]
    u_win[halo + tm:2 * halo + tm, :] = jnp.where(t < pl.num_programs(1) - 1, un_ref[...], 0.0)

    span = u_shift.shape[1]
    for k in range(1, SUBLANES):
        u_shift[k] = u_win[k:k + span, :]

    chunk = 64
    for i in range(tm // chunk):
        acc = jnp.zeros((chunk, conv_width), F32)
        for j in range(CONV_KERNEL):
            k = (halo - CONV_PAD + j) % SUBLANES
            start = i * chunk + (halo - CONV_PAD + j) - k
            tap = u_win[start:start + chunk, :] if k == 0 else u_shift[k, start:start + chunk, :]
            acc = acc + cw_ref[j:j + 1, :] * tap
        y = _layer_norm(acc + cb_ref[...], cg_ref[...], cbeta_ref[...])
        u_act[i * chunk:(i + 1) * chunk, :] = _silu(y).astype(BF16)

    attn = jnp.concatenate([attn_ref[i] for i in range(attn_ref.shape[0])], axis=1)
    mixed = jnp.dot(attn.astype(BF16), w_attn_ref[...], preferred_element_type=F32)
    mixed = mixed + jnp.dot(u_act[...], w_conv_ref[...], preferred_element_type=F32)
    o_ref[...] = _layer_norm(alpha * x_ref[...] + mixed, g1_ref[...], b1_ref[...])


def _mix(attn, u, x, w_attn, w_conv, conv_w, conv_b, conv_ln_g, conv_ln_b, ln_g, ln_b, alpha):
    B, S, D = x.shape
    n_pairs = attn.shape[1]
    attn_width = n_pairs * LANES
    conv_width = u.shape[-1]
    tm = ROW_TILE
    halo = 2 * SUBLANES
    assert halo >= CONV_PAD
    per_tile = tm // halo
    n_halo_blocks = S // halo
    row = lambda b, t: (b, t, 0)
    const = lambda b, t: (0, 0)
    return pl.pallas_call(
        functools.partial(_mix_kernel, alpha=alpha),
        grid=(B, S // tm),
        in_specs=[
            pl.BlockSpec((None, n_pairs, tm, LANES), lambda b, t: (b, 0, t, 0)),
            pl.BlockSpec((None, halo, conv_width), lambda b, t: (b, jnp.maximum(t * per_tile - 1, 0), 0)),
            pl.BlockSpec((None, tm, conv_width), row),
            pl.BlockSpec((None, halo, conv_width),
                         lambda b, t: (b, jnp.minimum((t + 1) * per_tile, n_halo_blocks - 1), 0)),
            pl.BlockSpec((None, tm, D), row),
            pl.BlockSpec((attn_width, D), const),
            pl.BlockSpec((conv_width, D), const),
            pl.BlockSpec((CONV_KERNEL, conv_width), const),
            pl.BlockSpec((1, conv_width), const),
            pl.BlockSpec((1, conv_width), const),
            pl.BlockSpec((1, conv_width), const),
            pl.BlockSpec((1, D), const),
            pl.BlockSpec((1, D), const),
        ],
        out_specs=pl.BlockSpec((None, tm, D), row),
        out_shape=jax.ShapeDtypeStruct((B, S, D), F32),
        scratch_shapes=[pltpu.VMEM((tm + 2 * halo, conv_width), F32),
                        pltpu.VMEM((SUBLANES, tm + 2 * halo - SUBLANES, conv_width), F32),
                        pltpu.VMEM((tm, conv_width), BF16)],
        compiler_params=pltpu.CompilerParams(
            dimension_semantics=("arbitrary", "arbitrary"), vmem_limit_bytes=VMEM_LIMIT_BYTES),
        name="mix",
    )(attn, u, u, u, x, w_attn, w_conv, conv_w, conv_b, conv_ln_g, conv_ln_b, ln_g, ln_b)


def _ffn_kernel(xp_ref, x_ref, xn_ref, wi_ref, wo_ref, cw_ref, cb_ref, g2_ref, b2_ref, o_ref,
                act_buf, *, alpha, chunk):
    t = pl.program_id(1)
    tm, d_ff = act_buf.shape
    x_t = x_ref[...]
    prev = jnp.where(t > 0, xp_ref[...], 0.0)
    nxt = jnp.where(t < pl.num_programs(1) - 1, xn_ref[...], 0.0)
    xb = x_t.astype(BF16)
    hb = jnp.concatenate([prev, x_t, nxt], axis=0).astype(BF16)
    rows = tm + 2 * SUBLANES
    mid = slice(SUBLANES, SUBLANES + tm)

    for c0 in range(0, d_ff, chunk):
        cols = slice(c0, c0 + chunk)
        g = jnp.dot(hb, wi_ref[:, cols], preferred_element_type=F32)
        up = jnp.dot(xb, wi_ref[:, d_ff + c0:d_ff + c0 + chunk], preferred_element_type=F32)
        gate = (cw_ref[0:1, cols] * pltpu.roll(g, 1, 0)[mid]
                + cw_ref[1:2, cols] * g[mid]
                + cw_ref[2:3, cols] * pltpu.roll(g, rows - 1, 0)[mid]
                + cb_ref[:, cols])
        act_buf[:, cols] = (_silu(gate) * up).astype(BF16)

    half = tm // 2
    for r0 in (0, half):
        ffn = jnp.dot(act_buf[r0:r0 + half, :], wo_ref[...], preferred_element_type=F32)
        o_ref[r0:r0 + half, :] = _layer_norm(alpha * x_t[r0:r0 + half] + ffn, g2_ref[...], b2_ref[...])


def _ffn(x1, w_in, w_out, conv_w, conv_b, ln_g, ln_b, alpha):
    B, S, D = x1.shape
    d_ff = w_out.shape[0]
    tm = ROW_TILE
    per_tile = tm // SUBLANES
    n_halo_blocks = S // SUBLANES
    row = lambda b, t: (b, t, 0)
    const2 = lambda b, t: (0, 0)
    return pl.pallas_call(
        functools.partial(_ffn_kernel, alpha=alpha, chunk=FFN_CHUNK),
        grid=(B, S // tm),
        in_specs=[
            pl.BlockSpec((None, SUBLANES, D), lambda b, t: (b, jnp.maximum(t * per_tile - 1, 0), 0)),
            pl.BlockSpec((None, tm, D), row),
            pl.BlockSpec((None, SUBLANES, D), lambda b, t: (b, jnp.minimum((t + 1) * per_tile, n_halo_blocks - 1), 0)),
            pl.BlockSpec((D, 2 * d_ff), const2),
            pl.BlockSpec((d_ff, D), const2),
            pl.BlockSpec((FFN_CONV_KERNEL, d_ff), const2),
            pl.BlockSpec((1, d_ff), const2),
            pl.BlockSpec((1, D), const2),
            pl.BlockSpec((1, D), const2),
        ],
        out_specs=pl.BlockSpec((None, tm, D), row),
        out_shape=jax.ShapeDtypeStruct((B, S, D), F32),
        scratch_shapes=[pltpu.VMEM((tm, d_ff), BF16)],
        compiler_params=pltpu.CompilerParams(
            dimension_semantics=("arbitrary", "arbitrary"), vmem_limit_bytes=VMEM_LIMIT_BYTES),
        name="ffn",
    )(x1, x1, x1, w_in, w_out, conv_w, conv_b, ln_g, ln_b)


def kernel(x, positions, w_in, b_glu, conv_w, conv_b, conv_ln_g, conv_ln_b, w_out, ln1_g, ln1_b,
           w_ffn_in, ffn_conv_w, ffn_conv_b, w_ffn_out, ln2_g, ln2_b):
    B, S, D = x.shape
    depth = w_in.shape[0]
    conv_width = conv_w.shape[-1]
    attn_width = w_out.shape[1] - conv_width
    d_ff = w_ffn_out.shape[1]
    alpha = (2.0 * depth) ** 0.25
    assert S % (GROUPS * QBLK) == 0 and S // GROUPS >= KWIN and S % ROW_TILE == 0
    assert attn_width % LANES == 0 and d_ff % FFN_CHUNK == 0 and w_in.shape[-1] == 3 * attn_width + 2 * conv_width

    inv_freq = ROPE_THETA ** (-jnp.arange(ROPE_HALF, dtype=F32) / ROPE_HALF)
    freq_lanes = inv_freq[jnp.arange(LANES) % ROPE_HALF][None, :]
    pos_lanes = jnp.broadcast_to(positions.astype(F32)[:, :, None], (B, S, LANES))

    for l in range(depth):
        q, k, v, u = _in_proj(x, pos_lanes, freq_lanes, w_in[l].astype(BF16), b_glu[l][None, :],
                              attn_width, conv_width)
        attn = _attention(q, k, v)
        w_o = w_out[l].astype(BF16)
        x = _mix(attn, u, x, w_o[:attn_width], w_o[attn_width:], conv_w[l], conv_b[l][None, :],
                 conv_ln_g[l][None, :], conv_ln_b[l][None, :], ln1_g[l][None, :], ln1_b[l][None, :], alpha)
        x = _ffn(x, w_ffn_in[l].astype(BF16), w_ffn_out[l].astype(BF16), ffn_conv_w[l], ffn_conv_b[l][None, :],
                 ln2_g[l][None, :], ln2_b[l][None, :], alpha)
    return x
```
